```python
import jax, jax.numpy as jnp
from jax import lax
import numpy as np

D_MODEL = 1024
BATCH = 16
SEQ = 4096
DEPTH = 2
DEC_BATCH = 4
DEC_SEQ = 4096
PAST_LEN = 128

HEAD_DIM = 64
N_HEADS_A = 8
N_HEADS_B = 8
N_KV_B = 2
MIX_WIDTH = (N_HEADS_A + N_HEADS_B) * HEAD_DIM
DILATED_PAIRS = ((128, 1), (512, 4), (2048, 16))
WINDOW_B = 128
D_FF = 2816
N_BUCKETS = 32
MAX_DISTANCE = 1024
EPS = 1e-6
NEG_INF = -1e30
IN_WIDTHS = (N_HEADS_A * HEAD_DIM, N_HEADS_A * HEAD_DIM, N_HEADS_A * HEAD_DIM,
             N_HEADS_B * HEAD_DIM, N_KV_B * HEAD_DIM, N_KV_B * HEAD_DIM)
W_IN_COLS = sum(IN_WIDTHS)

kernel_name = 'hybrid_dilated_window_encoder'


def rms_norm(x, g):
    x32 = x.astype(jnp.float32)
    y = x32 * lax.rsqrt(jnp.mean(x32 * x32, axis=-1, keepdims=True) + EPS)
    return (y * g.astype(jnp.float32)).astype(x.dtype)


def swiglu(x, w_gate, w_up, w_down):
    return (jax.nn.silu(x @ w_gate) * (x @ w_up)) @ w_down


def _rel_bucket(rel):
    nb = N_BUCKETS // 2
    max_exact = nb // 2
    n = np.abs(rel)
    large = max_exact + (np.log(np.maximum(n, 1).astype(np.float32) / max_exact)
                         / np.log(MAX_DISTANCE / max_exact) * (nb - max_exact)).astype(np.int32)
    large = np.minimum(large, nb - 1)
    return ((rel > 0).astype(np.int32) * nb + np.where(n < max_exact, n, large)).astype(np.int32)


def _band_bias(rel_table, head_lo, n_heads, win, dilation):
    qq = np.arange(win)[:, None]
    kk = np.arange(3 * win)[None, :]
    bucket = jnp.asarray(_rel_bucket((kk - win - qq) * dilation))
    b = rel_table[:, head_lo:head_lo + n_heads][bucket]
    return jnp.transpose(b, (2, 0, 1))


def band_attention(q, k, v, bias, win, sink=None):
    B, L, Hq, dh = q.shape
    Hk = k.shape[2]
    G = Hq // Hk
    nblk = -(-L // win)
    Lp = nblk * win
    qp = jnp.pad(q, ((0, 0), (0, Lp - L), (0, 0), (0, 0))).reshape(B, nblk, win, Hk, G, dh)
    pad_kv = ((0, 0), (win, Lp - L + win), (0, 0), (0, 0))

    def blocks3(t):
        t = jnp.pad(t, pad_kv).reshape(B, nblk + 2, win, Hk, dh)
        return jnp.concatenate([t[:, :-2], t[:, 1:-1], t[:, 2:]], axis=2)

    kb, vb = blocks3(k), blocks3(v)
    s = jnp.einsum('bnqhgd,bnkhd->bnhgqk', qp, kb, preferred_element_type=jnp.float32) * (dh ** -0.5)
    s = s + bias.reshape(Hk, G, win, 3 * win).astype(jnp.float32)
    qq = np.arange(win)[:, None]
    kk = np.arange(3 * win)[None, :]
    band = np.abs(kk - win - qq) <= win
    kpos = np.arange(nblk)[:, None] * win + kk - win
    inside = (kpos >= 0) & (kpos < L)
    mask = jnp.asarray(band[None, :, :] & inside[:, None, :])
    s = jnp.where(mask[None, :, None, None], s, NEG_INF)
    m = jnp.max(s, axis=-1)
    if sink is not None:
        sink_l = sink.astype(jnp.float32).reshape(Hk, G, 1)
        m = jnp.maximum(m, sink_l)
    p = jnp.exp(s - m[..., None])
    denom = jnp.sum(p, axis=-1)
    if sink is not None:
        denom = denom + jnp.exp(sink_l - m)
    o = jnp.einsum('bnhgqk,bnkhd->bnqhgd', (p / denom[..., None]).astype(v.dtype), vb)
    o = o.reshape(B, Lp, Hq, dh)[:, :L]
    lse = jnp.transpose(m + jnp.log(denom), (0, 1, 4, 2, 3)).reshape(B, Lp, Hq)[:, :L]
    return o, lse


def dilated_attention(q, k, v, rel_table):
    B, S, H, dh = q.shape
    outs, lses = [], []
    for window, d in DILATED_PAIRS:
        win = window // (2 * d)

        def to_sub(t):
            return jnp.transpose(t.reshape(B, S // d, d, H, dh), (0, 2, 1, 3, 4)).reshape(B * d, S // d, H, dh)

        bias = _band_bias(rel_table, 0, H, win, d)
        o, lse = band_attention(to_sub(q), to_sub(k), to_sub(v), bias, win)
        outs.append(jnp.transpose(o.reshape(B, d, S // d, H, dh), (0, 2, 1, 3, 4)).reshape(B, S, H, dh))
        lses.append(jnp.transpose(lse.reshape(B, d, S // d, H), (0, 2, 1, 3)).reshape(B, S, H))
    w = jax.nn.softmax(jnp.stack(lses, axis=0), axis=0)
    return jnp.einsum('ibsh,ibshd->bshd', w.astype(q.dtype), jnp.stack(outs, axis=0))


def encoder_layer(x, rel_table, ffn1_norm, ffn1_w_gate, ffn1_w_up, ffn1_w_down,
                  attn_norm, w_in, w_out, sink,
                  ffn2_norm, ffn2_w_gate, ffn2_w_up, ffn2_w_down):
    B, S, _ = x.shape
    x = x + 0.5 * swiglu(rms_norm(x, ffn1_norm), ffn1_w_gate, ffn1_w_up, ffn1_w_down)
    h = rms_norm(x, attn_norm)
    proj = h @ w_in
    offsets = np.cumsum(IN_WIDTHS)[:-1].tolist()
    qa, ka, va, qb, kb, vb = jnp.split(proj, offsets, axis=-1)
    qa = qa.reshape(B, S, N_HEADS_A, HEAD_DIM)
    ka = ka.reshape(B, S, N_HEADS_A, HEAD_DIM)
    va = va.reshape(B, S, N_HEADS_A, HEAD_DIM)
    qb = qb.reshape(B, S, N_HEADS_B, HEAD_DIM)
    kb = kb.reshape(B, S, N_KV_B, HEAD_DIM)
    vb = vb.reshape(B, S, N_KV_B, HEAD_DIM)
    oa = dilated_attention(qa, ka, va, rel_table)
    bias_b = _band_bias(rel_table, N_HEADS_A, N_HEADS_B, WINDOW_B, 1)
    ob, _ = band_attention(qb, kb, vb, bias_b, WINDOW_B, sink)
    o = jnp.concatenate([oa.reshape(B, S, -1), ob.reshape(B, S, -1)], axis=-1)
    x = x + o @ w_out
    x = x + 0.5 * swiglu(rms_norm(x, ffn2_norm), ffn2_w_gate, ffn2_w_up, ffn2_w_down)
    return x


def trunk(x, rel_table, ffn1_norm, ffn1_w_gate, ffn1_w_up, ffn1_w_down,
          attn_norm, w_in, w_out, sink,
          ffn2_norm, ffn2_w_gate, ffn2_w_up, ffn2_w_down, final_norm):
    for l in range(DEPTH):
        x = encoder_layer(x, rel_table, ffn1_norm[l], ffn1_w_gate[l], ffn1_w_up[l], ffn1_w_down[l],
                          attn_norm[l], w_in[l], w_out[l], sink[l],
                          ffn2_norm[l], ffn2_w_gate[l], ffn2_w_up[l], ffn2_w_down[l])
    return rms_norm(x, final_norm)


def setup_inputs(seed: int = 0) -> dict:
    key = jax.random.key(seed)
    ks = jax.random.split(key, 17)
    f32 = jnp.float32
    nrm = lambda k, shape, scale: jax.random.normal(k, shape, f32) * scale
    gain = lambda k, shape: 1.0 + 0.02 * jax.random.normal(k, shape, f32)
    return {
        'x_prompt': jax.random.normal(ks[0], (BATCH, SEQ, D_MODEL), f32),
        'x_sample': jax.random.normal(ks[1], (DEC_BATCH, DEC_SEQ, D_MODEL), f32),
        'rel_table': nrm(ks[2], (N_BUCKETS, N_HEADS_A + N_HEADS_B), 0.5),
        'ffn1_norm': gain(ks[3], (DEPTH, D_MODEL)),
        'ffn1_w_gate': nrm(ks[4], (DEPTH, D_MODEL, D_FF), D_MODEL ** -0.5),
        'ffn1_w_up': nrm(ks[5], (DEPTH, D_MODEL, D_FF), D_MODEL ** -0.5),
        'ffn1_w_down': nrm(ks[6], (DEPTH, D_FF, D_MODEL), D_FF ** -0.5),
        'attn_norm': gain(ks[7], (DEPTH, D_MODEL)),
        'w_in': nrm(ks[8], (DEPTH, D_MODEL, W_IN_COLS), D_MODEL ** -0.5),
        'w_out': nrm(ks[9], (DEPTH, MIX_WIDTH, D_MODEL), MIX_WIDTH ** -0.5),
        'sink': nrm(ks[10], (DEPTH, N_HEADS_B), 0.5),
        'ffn2_norm': gain(ks[11], (DEPTH, D_MODEL)),
        'ffn2_w_gate': nrm(ks[12], (DEPTH, D_MODEL, D_FF), D_MODEL ** -0.5),
        'ffn2_w_up': nrm(ks[13], (DEPTH, D_MODEL, D_FF), D_MODEL ** -0.5),
        'ffn2_w_down': nrm(ks[14], (DEPTH, D_FF, D_MODEL), D_FF ** -0.5),
        'final_norm': gain(ks[15], (D_MODEL,)),
    }


def reference(x_prompt, x_sample, rel_table, ffn1_norm, ffn1_w_gate, ffn1_w_up, ffn1_w_down,
              attn_norm, w_in, w_out, sink,
              ffn2_norm, ffn2_w_gate, ffn2_w_up, ffn2_w_down, final_norm):
    y_prompt = trunk(x_prompt, rel_table, ffn1_norm, ffn1_w_gate, ffn1_w_up, ffn1_w_down,
                     attn_norm, w_in, w_out, sink,
                     ffn2_norm, ffn2_w_gate, ffn2_w_up, ffn2_w_down, final_norm)
    y_sample = trunk(x_sample, rel_table, ffn1_norm, ffn1_w_gate, ffn1_w_up, ffn1_w_down,
                     attn_norm, w_in, w_out, sink,
                     ffn2_norm, ffn2_w_gate, ffn2_w_up, ffn2_w_down, final_norm)
    return (y_prompt, y_sample)
```

```python
import functools

import numpy as np
import jax
import jax.numpy as jnp
from jax import lax
from jax.experimental import pallas as pl
from jax.experimental.pallas import tpu as pltpu

HEAD_DIM = 64
N_HEADS_A = 8
N_HEADS_B = 8
N_KV_B = 2
DILATED_PAIRS = ((128, 1), (512, 4), (2048, 16))
WINDOW_B = 128
N_BUCKETS = 32
MAX_DISTANCE = 1024
EPS = 1e-6
NEG_INF = -1e30

Q_BLOCK = 128
TOKEN_TILE = 512
FF_CHUNK = 256
VMEM_LIMIT_BYTES = 56 * 1024 * 1024

_F32 = jnp.float32
_BF16 = jnp.bfloat16


def _rms(x, g):
    y = x * lax.rsqrt(jnp.mean(x * x, axis=-1, keepdims=True) + EPS)
    return y * g


def _swiglu_into(h, wg_ref, wu_ref, a_scr, d_ff):
    for c0 in range(0, d_ff, FF_CHUNK):
        c1 = min(c0 + FF_CHUNK, d_ff)
        g = jnp.dot(h, wg_ref[:, c0:c1], preferred_element_type=_F32)
        u = jnp.dot(h, wu_ref[:, c0:c1], preferred_element_type=_F32)
        a_scr[:, c0:c1] = (g * (1.0 / (1.0 + jnp.exp(-g))) * u).astype(_BF16)


def _const_spec(shape):
    nd = len(shape)
    return pl.BlockSpec(shape, lambda *_: (0,) * nd, pipeline_mode=pl.Buffered(1))


def _compiler_params(n_grid):
    return pltpu.CompilerParams(
        dimension_semantics=("arbitrary",) * n_grid,
        vmem_limit_bytes=VMEM_LIMIT_BYTES)


def _ffn_in_kernel(x_ref, g1_ref, wg_ref, wu_ref, wd_ref, g2_ref, win_ref,
                   x1_ref, qa_ref, ka_ref, va_ref, qb_ref, kb_ref, vb_ref, a_scr,
                   *, d_ff, widths, q_scale):
    x = x_ref[...]
    h = _rms(x, g1_ref[...]).astype(_BF16)
    _swiglu_into(h, wg_ref, wu_ref, a_scr, d_ff)
    y = jnp.dot(a_scr[...], wd_ref[...], preferred_element_type=_F32)
    x1 = x + 0.5 * y
    x1_ref[...] = x1
    h2 = _rms(x1, g2_ref[...]).astype(_BF16)
    outs = (qa_ref, ka_ref, va_ref, qb_ref, kb_ref, vb_ref)
    c0 = 0
    for i, (o_ref, w) in enumerate(zip(outs, widths)):
        p = jnp.dot(h2, win_ref[:, c0:c0 + w], preferred_element_type=_F32)
        if i in (0, 3):
            p = p * q_scale
        o_ref[...] = p.astype(_BF16)
        c0 += w


def _ffn_in(x, g1, wg, wu, wd, g2, w_in, widths):
    t, d = x.shape
    d_ff = wg.shape[1]
    tm = TOKEN_TILE
    tok = lambda w: pl.BlockSpec((tm, w), lambda i: (i, 0))
    kern = functools.partial(_ffn_in_kernel, d_ff=d_ff, widths=widths,
                             q_scale=HEAD_DIM ** -0.5)
    return pl.pallas_call(
        kern,
        grid=(t // tm,),
        in_specs=[tok(d), _const_spec((1, d)), _const_spec(wg.shape), _const_spec(wu.shape),
                  _const_spec(wd.shape), _const_spec((1, d)), _const_spec(w_in.shape)],
        out_specs=[tok(d)] + [tok(w) for w in widths],
        out_shape=[jax.ShapeDtypeStruct((t, d), _F32)]
        + [jax.ShapeDtypeStruct((t, w), _BF16) for w in widths],
        scratch_shapes=[pltpu.VMEM((tm, d_ff), _BF16)],
        compiler_params=_compiler_params(1),
        name="ffn_in",
    )(x, g1, wg, wu, wd, g2, w_in)


def _out_ffn_kernel(x_ref, oa_ref, ob_ref, woa_ref, wob_ref, g_ref, wg_ref, wu_ref, wd_ref,
                    gf_ref, y_ref, a_scr, *, d_ff, final_norm):
    x2 = (x_ref[...]
          + jnp.dot(oa_ref[...], woa_ref[...], preferred_element_type=_F32)
          + jnp.dot(ob_ref[...], wob_ref[...], preferred_element_type=_F32))
    h = _rms(x2, g_ref[...]).astype(_BF16)
    _swiglu_into(h, wg_ref, wu_ref, a_scr, d_ff)
    y = jnp.dot(a_scr[...], wd_ref[...], preferred_element_type=_F32)
    x3 = x2 + 0.5 * y
    if final_norm:
        x3 = _rms(x3, gf_ref[...])
    y_ref[...] = x3


def _out_ffn(x1, oa, ob, wo_a, wo_b, g, wg, wu, wd, gf, final_norm):
    t, d = x1.shape
    d_ff = wg.shape[1]
    tm = TOKEN_TILE
    tok = lambda w: pl.BlockSpec((tm, w), lambda i: (i, 0))
    kern = functools.partial(_out_ffn_kernel, d_ff=d_ff, final_norm=final_norm)
    return pl.pallas_call(
        kern,
        grid=(t // tm,),
        in_specs=[tok(d), tok(oa.shape[1]), tok(ob.shape[1]),
                  _const_spec(wo_a.shape), _const_spec(wo_b.shape), _const_spec((1, d)),
                  _const_spec(wg.shape), _const_spec(wu.shape), _const_spec(wd.shape),
                  _const_spec((1, d))],
        out_specs=tok(d),
        out_shape=jax.ShapeDtypeStruct((t, d), _F32),
        scratch_shapes=[pltpu.VMEM((tm, d_ff), _BF16)],
        compiler_params=_compiler_params(1),
        name="out_ffn",
    )(x1, oa, ob, wo_a, wo_b, g, wg, wu, wd, gf)


def _rel_bucket(rel):
    nb = N_BUCKETS // 2
    max_exact = nb // 2
    n = np.abs(rel)
    large = max_exact + (np.log(np.maximum(n, 1).astype(np.float32) / max_exact)
                         / np.log(MAX_DISTANCE / max_exact) * (nb - max_exact)).astype(np.int32)
    large = np.minimum(large, nb - 1)
    return ((rel > 0).astype(np.int32) * nb + np.where(n < max_exact, n, large)).astype(np.int32)


def _band_tables(rel_table, head_lo, n_heads, win, dilation):
    kw = Q_BLOCK + 2 * win
    i = np.arange(Q_BLOCK)[:, None]
    j = np.arange(kw)[None, :]
    tabs = []
    for off in (0, win, 2 * win):
        rel = j - i - off
        bucket = jnp.asarray(_rel_bucket(rel * dilation))
        b = rel_table[:, head_lo:head_lo + n_heads][bucket]
        b = jnp.where(jnp.asarray(np.abs(rel) <= win)[:, :, None], b, NEG_INF)
        tabs.append(jnp.transpose(b, (2, 0, 1)))
    return jnp.stack(tabs, axis=0).astype(_F32)


def _band_attn_kernel(*refs, n, win, n_q_heads, group, has_sink, merge, write_lse):
    it = iter(refs)
    q_ref, k_ref, v_ref, bias_ref = next(it), next(it), next(it), next(it)
    sink_ref = next(it) if has_sink else None
    if merge:
        o1_ref, l1_ref, o2_ref, l2_ref = next(it), next(it), next(it), next(it)
    o_ref = next(it)
    lse_ref = next(it) if write_lse else None

    kw = Q_BLOCK + 2 * win
    nq = n // Q_BLOCK
    dh = HEAD_DIM

    def q_block(qb, carry):
        qs = pl.multiple_of(qb * Q_BLOCK, Q_BLOCK)
        ks = pl.multiple_of(jnp.clip(qs - win, 0, n - kw), 64)
        var = jnp.where(qb == 0, 0, jnp.where(qb == nq - 1, 2, 1))
        for h in range(n_q_heads):
            hk = h // group
            q = q_ref[pl.ds(qs, Q_BLOCK), h * dh:(h + 1) * dh]
            k = k_ref[pl.ds(ks, kw), hk * dh:(hk + 1) * dh]
            v = v_ref[pl.ds(ks, kw), hk * dh:(hk + 1) * dh]
            s = lax.dot_general(q, k, (((1,), (1,)), ((), ())),
                                preferred_element_type=_F32)
            s = s + bias_ref[var, h]
            m = jnp.max(s, axis=-1, keepdims=True)
            if has_sink:
                m = jnp.maximum(m, sink_ref[h])
            p = jnp.exp(s - m)
            denom = jnp.sum(p, axis=-1, keepdims=True)
            if has_sink:
                denom = denom + jnp.exp(sink_ref[h] - m)
            o = jnp.dot(p.astype(_BF16), v, preferred_element_type=_F32) / denom
            cols = slice(h * dh, (h + 1) * dh)
            if write_lse or merge:
                lse = jnp.broadcast_to(m + jnp.log(denom), (Q_BLOCK, dh))
            if merge:
                l1 = l1_ref[pl.ds(qs, Q_BLOCK), cols]
                l2 = l2_ref[pl.ds(qs, Q_BLOCK), cols]
                mx = jnp.maximum(jnp.maximum(l1, l2), lse)
                e1 = jnp.exp(l1 - mx)
                e2 = jnp.exp(l2 - mx)
                e3 = jnp.exp(lse - mx)
                tot = e1 + e2 + e3
                o = ((e1 / tot) * o1_ref[pl.ds(qs, Q_BLOCK), cols].astype(_F32)
                     + (e2 / tot) * o2_ref[pl.ds(qs, Q_BLOCK), cols].astype(_F32)
                     + (e3 / tot) * o)
            o_ref[pl.ds(qs, Q_BLOCK), cols] = o.astype(o_ref.dtype)
            if write_lse:
                lse_ref[pl.ds(qs, Q_BLOCK), cols] = lse
        return carry

    lax.fori_loop(0, nq, q_block, 0)


def _band_attn(q, k, v, tables, *, dilation, win, group, sink=None, merge_with=None,
               write_lse=False):
    b, s, cq = q.shape
    ck = k.shape[2]
    d = dilation
    n = s // d
    view = lambda a: a.reshape(b, n, d * a.shape[2])
    blk = lambda c: pl.BlockSpec((None, n, c), lambda bi, r: (bi, 0, r))
    args = [view(q), view(k), view(v), tables]
    in_specs = [blk(cq), blk(ck), blk(ck), _const_spec(tables.shape)]
    if sink is not None:
        args.append(sink)
        in_specs.append(pl.BlockSpec(memory_space=pltpu.SMEM))
    if merge_with is not None:
        for a in merge_with:
            args.append(view(a))
            in_specs.append(blk(cq))
    out_shape = [jax.ShapeDtypeStruct((b, n, d * cq), _BF16)]
    out_specs = [blk(cq)]
    if write_lse:
        out_shape.append(jax.ShapeDtypeStruct((b, n, d * cq), _F32))
        out_specs.append(blk(cq))
    kern = functools.partial(
        _band_attn_kernel, n=n, win=win, n_q_heads=cq // HEAD_DIM, group=group,
        has_sink=sink is not None, merge=merge_with is not None, write_lse=write_lse)
    outs = pl.pallas_call(
        kern,
        grid=(b, d),
        in_specs=in_specs,
        out_specs=out_specs,
        out_shape=out_shape,
        compiler_params=_compiler_params(2),
        name=f"band_attn_d{d}_w{win}",
    )(*args)
    return [o.reshape(b, s, cq) for o in outs]


def _trunk(x, tables_a, tables_b, p):
    b, s, d = x.shape
    t = b * s
    xf = x.reshape(t, d)
    widths = (N_HEADS_A * HEAD_DIM,) * 3 + (N_HEADS_B * HEAD_DIM, N_KV_B * HEAD_DIM,
                                           N_KV_B * HEAD_DIM)
    depth = p["w_in"].shape[0]
    for l in range(depth):
        x1, qa, ka, va, qb, kb, vb = _ffn_in(
            xf, p["ffn1_norm"][l], p["ffn1_w_gate"][l], p["ffn1_w_up"][l], p["ffn1_w_down"][l],
            p["attn_norm"][l], p["w_in"][l], widths)
        sh = lambda a: a.reshape(b, s, a.shape[1])
        qa, ka, va, qb, kb, vb = map(sh, (qa, ka, va, qb, kb, vb))
        (w1, d1), (w2, d2), (w3, d3) = DILATED_PAIRS
        o1, l1 = _band_attn(qa, ka, va, tables_a[0], dilation=d1, win=w1 // (2 * d1), group=1,
                            write_lse=True)
        o2, l2 = _band_attn(qa, ka, va, tables_a[1], dilation=d2, win=w2 // (2 * d2), group=1,
                            write_lse=True)
        (oa,) = _band_attn(qa, ka, va, tables_a[2], dilation=d3, win=w3 // (2 * d3), group=1,
                           merge_with=(o1, l1, o2, l2))
        (ob,) = _band_attn(qb, kb, vb, tables_b, dilation=1, win=WINDOW_B,
                           group=N_HEADS_B // N_KV_B, sink=p["sink"][l])
        xf = _out_ffn(x1, oa.reshape(t, -1), ob.reshape(t, -1), p["w_out_a"][l], p["w_out_b"][l],
                      p["ffn2_norm"][l], p["ffn2_w_gate"][l], p["ffn2_w_up"][l],
                      p["ffn2_w_down"][l], p["final_norm"], final_norm=(l == depth - 1))
    return xf.reshape(b, s, d)


def kernel(x_prompt, x_sample, rel_table, ffn1_norm, ffn1_w_gate, ffn1_w_up, ffn1_w_down,
           attn_norm, w_in, w_out, sink, ffn2_norm, ffn2_w_gate, ffn2_w_up, ffn2_w_down,
           final_norm):
    depth, d = ffn1_norm.shape
    mix_a = N_HEADS_A * HEAD_DIM
    p = {
        "ffn1_norm": ffn1_norm.reshape(depth, 1, d),
        "ffn1_w_gate": ffn1_w_gate.astype(_BF16),
        "ffn1_w_up": ffn1_w_up.astype(_BF16),
        "ffn1_w_down": ffn1_w_down.astype(_BF16),
        "attn_norm": attn_norm.reshape(depth, 1, d),
        "w_in": w_in.astype(_BF16),
        "w_out_a": w_out[:, :mix_a].astype(_BF16),
        "w_out_b": w_out[:, mix_a:].astype(_BF16),
        "sink": sink.astype(_F32),
        "ffn2_norm": ffn2_norm.reshape(depth, 1, d),
        "ffn2_w_gate": ffn2_w_gate.astype(_BF16),
        "ffn2_w_up": ffn2_w_up.astype(_BF16),
        "ffn2_w_down": ffn2_w_down.astype(_BF16),
        "final_norm": final_norm.reshape(1, d),
    }
    tables_a = [_band_tables(rel_table, 0, N_HEADS_A, w // (2 * dd), dd)
                for w, dd in DILATED_PAIRS]
    tables_b = _band_tables(rel_table, N_HEADS_A, N_HEADS_B, WINDOW_B, 1)
    return (_trunk(x_prompt, tables_a, tables_b, p),
            _trunk(x_sample, tables_a, tables_b, p))
```

```python
import functools

import numpy as np
import jax
import jax.numpy as jnp
from jax import lax
from jax.experimental import pallas as pl
from jax.experimental.pallas import tpu as pltpu

HEAD_DIM = 64
N_HEADS_A = 8
N_HEADS_B = 8
N_KV_B = 2
DILATED_PAIRS = ((128, 1), (512, 4), (2048, 16))
WINDOW_B = 128
N_BUCKETS = 32
MAX_DISTANCE = 1024
EPS = 1e-6
NEG_INF = -1e30

LANES = 128
Q_BLOCK = 128
TOKEN_TILE = 512
FF_CHUNK = 256
UNITS_PER_ITER = 8
VMEM_LIMIT_BYTES = 56 * 1024 * 1024

_F32 = jnp.float32
_BF16 = jnp.bfloat16


def _rms(x, g):
    y = x * lax.rsqrt(jnp.mean(x * x, axis=-1, keepdims=True) + EPS)
    return y * g


def _swiglu_into(h, wg_ref, wu_ref, a_scr, d_ff):
    for c0 in range(0, d_ff, FF_CHUNK):
        c1 = min(c0 + FF_CHUNK, d_ff)
        g = jnp.dot(h, wg_ref[:, c0:c1], preferred_element_type=_F32)
        u = jnp.dot(h, wu_ref[:, c0:c1], preferred_element_type=_F32)
        a_scr[:, c0:c1] = (g * (1.0 / (1.0 + jnp.exp(-g))) * u).astype(_BF16)


def _const_spec(shape):
    nd = len(shape)
    return pl.BlockSpec(shape, lambda *_: (0,) * nd, pipeline_mode=pl.Buffered(1))


def _compiler_params(n_grid):
    return pltpu.CompilerParams(
        dimension_semantics=("arbitrary",) * n_grid,
        vmem_limit_bytes=VMEM_LIMIT_BYTES)


def _ffn_in_kernel(x_ref, g1_ref, wg_ref, wu_ref, wd_ref, g2_ref, win_ref,
                   x1_ref, *rest, d_ff, widths):
    out_refs, a_scr = rest[:-1], rest[-1]
    x = x_ref[...]
    h = _rms(x, g1_ref[...]).astype(_BF16)
    _swiglu_into(h, wg_ref, wu_ref, a_scr, d_ff)
    y = jnp.dot(a_scr[...], wd_ref[...], preferred_element_type=_F32)
    x1 = x + 0.5 * y
    x1_ref[...] = x1
    h2 = _rms(x1, g2_ref[...]).astype(_BF16)
    c0 = 0
    for o_ref, w in zip(out_refs, widths):
        p = jnp.dot(h2, win_ref[:, c0:c0 + w], preferred_element_type=_F32)
        o_ref[...] = p.astype(o_ref.dtype)
        c0 += w


def _ffn_in(x, g1, wg, wu, wd, g2, w_in, widths, dtypes):
    t, d = x.shape
    d_ff = wg.shape[1]
    tm = TOKEN_TILE
    tok = lambda w: pl.BlockSpec((tm, w), lambda i: (i, 0))
    kern = functools.partial(_ffn_in_kernel, d_ff=d_ff, widths=widths)
    return pl.pallas_call(
        kern,
        grid=(t // tm,),
        in_specs=[tok(d), _const_spec((1, d)), _const_spec(wg.shape), _const_spec(wu.shape),
                  _const_spec(wd.shape), _const_spec((1, d)), _const_spec(w_in.shape)],
        out_specs=[tok(d)] + [tok(w) for w in widths],
        out_shape=[jax.ShapeDtypeStruct((t, d), _F32)]
        + [jax.ShapeDtypeStruct((t, w), dt) for w, dt in zip(widths, dtypes)],
        scratch_shapes=[pltpu.VMEM((tm, d_ff), _BF16)],
        compiler_params=_compiler_params(1),
        name="ffn_in",
    )(x, g1, wg, wu, wd, g2, w_in)


def _out_ffn_kernel(x_ref, oa_ref, ob_ref, woa_ref, wob_ref, g_ref, wg_ref, wu_ref, wd_ref,
                    gf_ref, y_ref, a_scr, *, d_ff, final_norm):
    x2 = (x_ref[...]
          + jnp.dot(oa_ref[...].astype(_BF16), woa_ref[...], preferred_element_type=_F32)
          + jnp.dot(ob_ref[...], wob_ref[...], preferred_element_type=_F32))
    h = _rms(x2, g_ref[...]).astype(_BF16)
    _swiglu_into(h, wg_ref, wu_ref, a_scr, d_ff)
    y = jnp.dot(a_scr[...], wd_ref[...], preferred_element_type=_F32)
    x3 = x2 + 0.5 * y
    if final_norm:
        x3 = _rms(x3, gf_ref[...])
    y_ref[...] = x3


def _out_ffn(x1, oa, ob, wo_a, wo_b, g, wg, wu, wd, gf, final_norm):
    t, d = x1.shape
    d_ff = wg.shape[1]
    tm = TOKEN_TILE
    tok = lambda w: pl.BlockSpec((tm, w), lambda i: (i, 0))
    kern = functools.partial(_out_ffn_kernel, d_ff=d_ff, final_norm=final_norm)
    return pl.pallas_call(
        kern,
        grid=(t // tm,),
        in_specs=[tok(d), tok(oa.shape[1]), tok(ob.shape[1]),
                  _const_spec(wo_a.shape), _const_spec(wo_b.shape), _const_spec((1, d)),
                  _const_spec(wg.shape), _const_spec(wu.shape), _const_spec(wd.shape),
                  _const_spec((1, d))],
        out_specs=tok(d),
        out_shape=jax.ShapeDtypeStruct((t, d), _F32),
        scratch_shapes=[pltpu.VMEM((tm, d_ff), _BF16)],
        compiler_params=_compiler_params(1),
        name="out_ffn",
    )(x1, oa, ob, wo_a, wo_b, g, wg, wu, wd, gf)


def _rel_bucket(rel):
    nb = N_BUCKETS // 2
    max_exact = nb // 2
    n = np.abs(rel)
    large = max_exact + (np.log(np.maximum(n, 1).astype(np.float32) / max_exact)
                         / np.log(MAX_DISTANCE / max_exact) * (nb - max_exact)).astype(np.int32)
    large = np.minimum(large, nb - 1)
    return ((rel > 0).astype(np.int32) * nb + np.where(n < max_exact, n, large)).astype(np.int32)


def _band_tables(rel_table, head_lo, n_heads, win, dilation):
    kw = Q_BLOCK + 2 * win
    i = np.arange(Q_BLOCK)[:, None]
    j = np.arange(kw)[None, :]
    tab = rel_table[:, head_lo:head_lo + n_heads].astype(_F32)
    out = []
    for off in (0, win, 2 * win):
        rel = j - i - off
        bucket = jnp.asarray(_rel_bucket(rel * dilation))[None]
        b = jnp.zeros((n_heads, Q_BLOCK, kw), _F32)
        for kb in range(N_BUCKETS):
            b = jnp.where(bucket == kb, tab[kb][:, None, None], b)
        out.append(jnp.where(jnp.asarray(np.abs(rel) <= win)[None], b, NEG_INF))
    return jnp.stack(out, axis=0)


def _lane_masks():
    lane = lax.broadcasted_iota(jnp.int32, (1, LANES), 1)
    lo = lane < HEAD_DIM
    return lo, (lo.astype(_BF16), (~lo).astype(_BF16))


def _pair_unit(q2, k2, v_aug, biases, qmasks, sinks=None):
    res = []
    for h in range(2):
        qm = q2 * qmasks[h]
        s = lax.dot_general(qm, k2, (((1,), (1,)), ((), ())), preferred_element_type=_F32)
        s = s + biases[h]
        m = jnp.max(s, axis=-1, keepdims=True)
        if sinks is not None:
            m = jnp.maximum(m, sinks[h])
        p = jnp.exp(s - m).astype(_BF16)
        pv = jnp.dot(p, v_aug, preferred_element_type=_F32)
        res.append((pv[:, :LANES], pv[:, LANES:], m))
    return res


def _variant(qb, nq):
    return jnp.where(qb == 0, 0, jnp.where(qb == nq - 1, 2, 1))


def _loop(n, body, per_iter):
    assert n % per_iter == 0

    def outer(io, c):
        for u in range(per_iter):
            body(io * per_iter + u)
        return c

    lax.fori_loop(0, n // per_iter, outer, 0)


def _attn_a_kernel(q_ref, k_ref, v_ref, bias_ref, o_ref,
                   q4, k4, v4, o1, m1, d1, o2, m2, d2, o3, m3, d3, *, s_len, win):
    kw = Q_BLOCK + 2 * win
    n4 = s_len // 4
    lo, qmasks = _lane_masks()
    ones = jnp.ones((kw, LANES), _BF16)

    def unit(q2, k2, v2, branch, var):
        v_aug = jnp.concatenate([v2.astype(_BF16), ones], axis=1)
        r = _pair_unit(q2.astype(_BF16), k2.astype(_BF16), v_aug,
                       (bias_ref[branch, var, 0], bias_ref[branch, var, 1]), qmasks)
        (oa_, da_, ma_), (ob_, db_, mb_) = r
        return (jnp.where(lo, oa_, ob_), jnp.where(lo, ma_, mb_), jnp.where(lo, da_, db_))

    def window(qb, n):
        qs = qb * Q_BLOCK
        ks = jnp.clip(qs - win, 0, n - kw)
        return qs, ks

    nq1 = s_len // Q_BLOCK

    def b1(qb):
        qs, ks = window(qb, s_len)
        qs = pl.multiple_of(qs, Q_BLOCK)
        ks = pl.multiple_of(ks, 64)
        o, m, dn = unit(q_ref[pl.ds(qs, Q_BLOCK), :], k_ref[pl.ds(ks, kw), :],
                        v_ref[pl.ds(ks, kw), :], 0, _variant(qb, nq1))
        o1[pl.ds(qs, Q_BLOCK), :] = o
        m1[pl.ds(qs, Q_BLOCK), :] = m
        d1[pl.ds(qs, Q_BLOCK), :] = dn

    _loop(nq1, b1, UNITS_PER_ITER)

    blk = 256
    nb4 = n4 // blk

    def to4(i):
        r4 = i // nb4
        u0 = (i % nb4) * blk
        dst = pl.multiple_of(r4 * n4 + u0, blk)
        for src, dst_ref in ((q_ref, q4), (k_ref, k4), (v_ref, v4)):
            dst_ref[pl.ds(dst, blk), :] = src[pl.ds(r4 + 4 * u0, blk, stride=4), :]

    _loop(4 * nb4, to4, 1)

    nq2 = n4 // Q_BLOCK

    def b2(i):
        r4 = i // nq2
        qb = i % nq2
        qs, ks = window(qb, n4)
        qs = pl.multiple_of(r4 * n4 + qs, Q_BLOCK)
        ks = pl.multiple_of(r4 * n4 + ks, 64)
        o, m, dn = unit(q4[pl.ds(qs, Q_BLOCK), :], k4[pl.ds(ks, kw), :],
                        v4[pl.ds(ks, kw), :], 1, _variant(qb, nq2))
        o2[pl.ds(qs, Q_BLOCK), :] = o
        m2[pl.ds(qs, Q_BLOCK), :] = m
        d2[pl.ds(qs, Q_BLOCK), :] = dn

    _loop(4 * nq2, b2, UNITS_PER_ITER)

    n16 = s_len // 16
    nq3 = n16 // Q_BLOCK

    def b3(i):
        r16 = i // nq3
        qb = i % nq3
        base = (r16 % 4) * n4 + r16 // 4
        qs, ks = window(qb, n16)
        qrow = base + 4 * qs
        krow = base + 4 * ks
        o, m, dn = unit(q4[pl.ds(qrow, Q_BLOCK, stride=4), :], k4[pl.ds(krow, kw, stride=4), :],
                        v4[pl.ds(krow, kw, stride=4), :], 2, _variant(qb, nq3))
        o3[pl.ds(qrow, Q_BLOCK, stride=4), :] = o
        m3[pl.ds(qrow, Q_BLOCK, stride=4), :] = m
        d3[pl.ds(qrow, Q_BLOCK, stride=4), :] = dn

    _loop(16 * nq3, b3, UNITS_PER_ITER)

    def mix(i):
        r4 = i // nq2
        u0 = (i % nq2) * Q_BLOCK
        row4 = pl.multiple_of(r4 * n4 + u0, Q_BLOCK)
        nat = pl.ds(r4 + 4 * u0, Q_BLOCK, stride=4)
        by4 = pl.ds(row4, Q_BLOCK)
        ma, mb, mc = m1[nat, :], m2[by4, :], m3[by4, :]
        mx = jnp.maximum(jnp.maximum(ma, mb), mc)
        ea, eb, ec = jnp.exp(ma - mx), jnp.exp(mb - mx), jnp.exp(mc - mx)
        num = ea * o1[nat, :] + eb * o2[by4, :] + ec * o3[by4, :]
        den = ea * d1[nat, :] + eb * d2[by4, :] + ec * d3[by4, :]
        o_ref[nat, :] = num / den

    _loop(4 * nq2, mix, 2)


def _attn_a(qa, ka, va, tables):
    b, s, c = qa.shape
    n_pairs = c // LANES
    win = DILATED_PAIRS[0][0] // 2
    kw = Q_BLOCK + 2 * win
    blk = pl.BlockSpec((None, s, LANES), lambda bi, p: (bi, 0, p))
    bias_spec = pl.BlockSpec((3, 3, 2, Q_BLOCK, kw), lambda bi, p: (0, 0, p, 0, 0))
    kern = functools.partial(_attn_a_kernel, s_len=s, win=win)
    return pl.pallas_call(
        kern,
        grid=(b, n_pairs),
        in_specs=[blk, blk, blk, bias_spec],
        out_specs=blk,
        out_shape=jax.ShapeDtypeStruct((b, s, c), _F32),
        scratch_shapes=[pltpu.VMEM((s, LANES), _F32) for _ in range(12)],
        compiler_params=_compiler_params(2),
        name="attn_a",
    )(qa, ka, va, tables)


def _attn_b_kernel(q_ref, k_ref, v_ref, bias_ref, sink_ref, o_ref, *, s_len, win, group):
    kw = Q_BLOCK + 2 * win
    nq = s_len // Q_BLOCK
    lo, qmasks = _lane_masks()
    ones = jnp.ones((kw, LANES), _BF16)
    g = pl.program_id(1)

    def body(qb):
        qs = pl.multiple_of(qb * Q_BLOCK, Q_BLOCK)
        ks = pl.multiple_of(jnp.clip(qs - win, 0, s_len - kw), 64)
        var = _variant(qb, nq)
        k2 = k_ref[pl.ds(ks, kw), :]
        v_aug = jnp.concatenate([v_ref[pl.ds(ks, kw), :], ones], axis=1)
        for j in range(group // 2):
            cols = slice(j * LANES, (j + 1) * LANES)
            sinks = (sink_ref[g * group + 2 * j], sink_ref[g * group + 2 * j + 1])
            r = _pair_unit(q_ref[pl.ds(qs, Q_BLOCK), cols], k2, v_aug,
                           (bias_ref[var, 2 * j], bias_ref[var, 2 * j + 1]), qmasks, sinks)
            (oa_, da_, ma_), (ob_, db_, mb_) = r
            o = jnp.where(lo, oa_, ob_)
            m = jnp.where(lo, ma_, mb_)
            sk = jnp.where(lo, sinks[0], sinks[1])
            den = jnp.where(lo, da_, db_) + jnp.exp(sk - m)
            o_ref[pl.ds(qs, Q_BLOCK), cols] = (o / den).astype(o_ref.dtype)

    _loop(nq, body, UNITS_PER_ITER // 2)


def _attn_b(qb, kbd, vbd, tables, sink):
    b, s, c = qb.shape
    group = N_HEADS_B // N_KV_B
    gw = group * HEAD_DIM
    kw = Q_BLOCK + 2 * WINDOW_B
    qspec = pl.BlockSpec((None, s, gw), lambda bi, g: (bi, 0, g))
    kspec = pl.BlockSpec((None, s, LANES), lambda bi, g: (bi, 0, g))
    bias_spec = pl.BlockSpec((3, group, Q_BLOCK, kw), lambda bi, g: (0, g, 0, 0))
    kern = functools.partial(_attn_b_kernel, s_len=s, win=WINDOW_B, group=group)
    return pl.pallas_call(
        kern,
        grid=(b, N_KV_B),
        in_specs=[qspec, kspec, kspec, bias_spec, pl.BlockSpec(memory_space=pltpu.SMEM)],
        out_specs=qspec,
        out_shape=jax.ShapeDtypeStruct((b, s, c), _BF16),
        compiler_params=_compiler_params(2),
        name="attn_b",
    )(qb, kbd, vbd, tables, sink)


_MIX_A = N_HEADS_A * HEAD_DIM
_MIX_B = N_HEADS_B * HEAD_DIM
_PROJ_WIDTHS = (_MIX_A, _MIX_A, _MIX_A, _MIX_B, N_KV_B * LANES, N_KV_B * LANES)
_PROJ_DTYPES = (_F32, _F32, _F32, _BF16, _BF16, _BF16)


def _prep_w_in(w_in):
    scale = HEAD_DIM ** -0.5
    kvw = N_KV_B * HEAD_DIM
    offs = np.cumsum((0, _MIX_A, _MIX_A, _MIX_A, _MIX_B, kvw, kvw))
    qa, ka, va, qb, kb, vb = (w_in[:, :, offs[i]:offs[i + 1]] for i in range(6))

    def dup(w):
        heads = [w[:, :, h * HEAD_DIM:(h + 1) * HEAD_DIM] for h in range(N_KV_B)]
        return jnp.concatenate([x for h in heads for x in (h, h)], axis=-1)

    return jnp.concatenate([qa * scale, ka, va, qb * scale, dup(kb), dup(vb)],
                           axis=-1).astype(_BF16)


def _trunk(x, tables_a, tables_b, p):
    b, s, d = x.shape
    t = b * s
    xf = x.reshape(t, d)
    depth = p["w_in"].shape[0]
    for l in range(depth):
        x1, qa, ka, va, qb, kbd, vbd = _ffn_in(
            xf, p["ffn1_norm"][l], p["ffn1_w_gate"][l], p["ffn1_w_up"][l], p["ffn1_w_down"][l],
            p["attn_norm"][l], p["w_in"][l], _PROJ_WIDTHS, _PROJ_DTYPES)
        sh = lambda a: a.reshape(b, s, a.shape[1])
        oa = _attn_a(sh(qa), sh(ka), sh(va), tables_a)
        ob = _attn_b(sh(qb), sh(kbd), sh(vbd), tables_b, p["sink"][l])
        xf = _out_ffn(x1, oa.reshape(t, -1), ob.reshape(t, -1), p["w_out_a"][l], p["w_out_b"][l],
                      p["ffn2_norm"][l], p["ffn2_w_gate"][l], p["ffn2_w_up"][l],
                      p["ffn2_w_down"][l], p["final_norm"], final_norm=(l == depth - 1))
    return xf.reshape(b, s, d)


def kernel(x_prompt, x_sample, rel_table, ffn1_norm, ffn1_w_gate, ffn1_w_up, ffn1_w_down,
           attn_norm, w_in, w_out, sink, ffn2_norm, ffn2_w_gate, ffn2_w_up, ffn2_w_down,
           final_norm):
    depth, d = ffn1_norm.shape
    assert all(w // (2 * dd) == DILATED_PAIRS[0][0] // 2 for w, dd in DILATED_PAIRS)
    assert tuple(dd for _, dd in DILATED_PAIRS) == (1, 4, 16)
    p = {
        "ffn1_norm": ffn1_norm.reshape(depth, 1, d),
        "ffn1_w_gate": ffn1_w_gate.astype(_BF16),
        "ffn1_w_up": ffn1_w_up.astype(_BF16),
        "ffn1_w_down": ffn1_w_down.astype(_BF16),
        "attn_norm": attn_norm.reshape(depth, 1, d),
        "w_in": _prep_w_in(w_in),
        "w_out_a": w_out[:, :_MIX_A].astype(_BF16),
        "w_out_b": w_out[:, _MIX_A:].astype(_BF16),
        "sink": sink.astype(_F32),
        "ffn2_norm": ffn2_norm.reshape(depth, 1, d),
        "ffn2_w_gate": ffn2_w_gate.astype(_BF16),
        "ffn2_w_up": ffn2_w_up.astype(_BF16),
        "ffn2_w_down": ffn2_w_down.astype(_BF16),
        "final_norm": final_norm.reshape(1, d),
    }
    tables_a = jnp.stack([_band_tables(rel_table, 0, N_HEADS_A, w // (2 * dd), dd)
                          for w, dd in DILATED_PAIRS], axis=0)
    tables_b = _band_tables(rel_table, N_HEADS_A, N_HEADS_B, WINDOW_B, 1)
    return (_trunk(x_prompt, tables_a, tables_b, p),
            _trunk(x_sample, tables_a, tables_b, p))
```

```python
import functools

import numpy as np
import jax
import jax.numpy as jnp
from jax import lax
from jax.experimental import pallas as pl
from jax.experimental.pallas import tpu as pltpu

HEAD_DIM = 64
N_HEADS_A = 8
N_HEADS_B = 8
N_KV_B = 2
DILATED_PAIRS = ((128, 1), (512, 4), (2048, 16))
WINDOW_B = 128
N_BUCKETS = 32
MAX_DISTANCE = 1024
EPS = 1e-6
NEG_INF = -1e30

LANES = 128
Q_BLOCK = 128
TOKEN_TILE = 512
FF_CHUNK = 256
UNITS_PER_ITER = 16
VMEM_LIMIT_BYTES = 56 * 1024 * 1024

_F32 = jnp.float32
_BF16 = jnp.bfloat16


def _rms(x, g):
    y = x * lax.rsqrt(jnp.mean(x * x, axis=-1, keepdims=True) + EPS)
    return y * g


def _swiglu_into(h, wg_ref, wu_ref, a_scr, d_ff):
    for c0 in range(0, d_ff, FF_CHUNK):
        c1 = min(c0 + FF_CHUNK, d_ff)
        g = jnp.dot(h, wg_ref[:, c0:c1], preferred_element_type=_F32)
        u = jnp.dot(h, wu_ref[:, c0:c1], preferred_element_type=_F32)
        a_scr[:, c0:c1] = (g * (1.0 / (1.0 + jnp.exp(-g))) * u).astype(_BF16)


def _const_spec(shape):
    nd = len(shape)
    return pl.BlockSpec(shape, lambda *_: (0,) * nd, pipeline_mode=pl.Buffered(1))


def _layer_spec(arr, layer):
    nd = arr.ndim - 1
    return pl.BlockSpec((None,) + arr.shape[1:], lambda *_: (layer,) + (0,) * nd,
                        pipeline_mode=pl.Buffered(1))


def _compiler_params(n_grid):
    return pltpu.CompilerParams(
        dimension_semantics=("arbitrary",) * n_grid,
        vmem_limit_bytes=VMEM_LIMIT_BYTES)


def _ffn_in_kernel(x_ref, g1_ref, wg_ref, wu_ref, wd_ref, g2_ref, win_ref,
                   x1_ref, *rest, d_ff, widths):
    out_refs, a_scr = rest[:-1], rest[-1]
    x = x_ref[...]
    h = _rms(x, g1_ref[...]).astype(_BF16)
    _swiglu_into(h, wg_ref, wu_ref, a_scr, d_ff)
    y = jnp.dot(a_scr[...], wd_ref[...], preferred_element_type=_F32)
    x1 = x + 0.5 * y
    x1_ref[...] = x1
    h2 = _rms(x1, g2_ref[...]).astype(_BF16)
    c0 = 0
    for o_ref, w in zip(out_refs, widths):
        p = jnp.dot(h2, win_ref[:, c0:c0 + w], preferred_element_type=_F32)
        o_ref[...] = p.astype(o_ref.dtype)
        c0 += w


def _ffn_in(x, layer, g1, wg, wu, wd, g2, w_in, widths, dtypes):
    t, d = x.shape
    d_ff = wg.shape[-1]
    tm = TOKEN_TILE
    tok = lambda w: pl.BlockSpec((tm, w), lambda i: (i, 0))
    kern = functools.partial(_ffn_in_kernel, d_ff=d_ff, widths=widths)
    return pl.pallas_call(
        kern,
        grid=(t // tm,),
        in_specs=[tok(d)] + [_layer_spec(a, layer) for a in (g1, wg, wu, wd, g2, w_in)],
        out_specs=[tok(d)] + [tok(w) for w in widths],
        out_shape=[jax.ShapeDtypeStruct((t, d), _F32)]
        + [jax.ShapeDtypeStruct((t, w), dt) for w, dt in zip(widths, dtypes)],
        scratch_shapes=[pltpu.VMEM((tm, d_ff), _BF16)],
        compiler_params=_compiler_params(1),
        name="ffn_in",
    )(x, g1, wg, wu, wd, g2, w_in)


def _out_ffn_kernel(x_ref, oa_ref, ob_ref, woa_ref, wob_ref, g_ref, wg_ref, wu_ref, wd_ref,
                    gf_ref, y_ref, a_scr, *, d_ff, final_norm):
    x2 = (x_ref[...]
          + jnp.dot(oa_ref[...].astype(_BF16), woa_ref[...], preferred_element_type=_F32)
          + jnp.dot(ob_ref[...], wob_ref[...], preferred_element_type=_F32))
    h = _rms(x2, g_ref[...]).astype(_BF16)
    _swiglu_into(h, wg_ref, wu_ref, a_scr, d_ff)
    y = jnp.dot(a_scr[...], wd_ref[...], preferred_element_type=_F32)
    x3 = x2 + 0.5 * y
    if final_norm:
        x3 = _rms(x3, gf_ref[...])
    y_ref[...] = x3


def _out_ffn(x1, oa, ob, layer, wo_a, wo_b, g, wg, wu, wd, gf, final_norm):
    t, d = x1.shape
    d_ff = wg.shape[-1]
    tm = TOKEN_TILE
    tok = lambda w: pl.BlockSpec((tm, w), lambda i: (i, 0))
    kern = functools.partial(_out_ffn_kernel, d_ff=d_ff, final_norm=final_norm)
    return pl.pallas_call(
        kern,
        grid=(t // tm,),
        in_specs=[tok(d), tok(oa.shape[1]), tok(ob.shape[1])]
        + [_layer_spec(a, layer) for a in (wo_a, wo_b, g, wg, wu, wd)]
        + [_const_spec((1, d))],
        out_specs=tok(d),
        out_shape=jax.ShapeDtypeStruct((t, d), _F32),
        scratch_shapes=[pltpu.VMEM((tm, d_ff), _BF16)],
        compiler_params=_compiler_params(1),
        name="out_ffn",
    )(x1, oa, ob, wo_a, wo_b, g, wg, wu, wd, gf)


def _rel_bucket(rel):
    nb = N_BUCKETS // 2
    max_exact = nb // 2
    n = np.abs(rel)
    large = max_exact + (np.log(np.maximum(n, 1).astype(np.float32) / max_exact)
                         / np.log(MAX_DISTANCE / max_exact) * (nb - max_exact)).astype(np.int32)
    large = np.minimum(large, nb - 1)
    return ((rel > 0).astype(np.int32) * nb + np.where(n < max_exact, n, large)).astype(np.int32)


def _band_tables(rel_table, head_lo, n_heads, win, dilation):
    kw = Q_BLOCK + 2 * win
    i = np.arange(Q_BLOCK)[:, None]
    j = np.arange(kw)[None, :]
    tab = rel_table[:, head_lo:head_lo + n_heads].astype(_F32)
    out = []
    for off in (0, win, 2 * win):
        rel = j - i - off
        bucket = jnp.asarray(_rel_bucket(rel * dilation))[None]
        b = jnp.zeros((n_heads, Q_BLOCK, kw), _F32)
        for kb in range(N_BUCKETS):
            b = jnp.where(bucket == kb, tab[kb][:, None, None], b)
        out.append(jnp.where(jnp.asarray(np.abs(rel) <= win)[None], b, NEG_INF))
    return jnp.stack(out, axis=0)


def _lane_masks():
    lane = lax.broadcasted_iota(jnp.int32, (1, LANES), 1)
    lo = lane < HEAD_DIM
    return lo, (lo.astype(_BF16), (~lo).astype(_BF16))


def _pair_unit(q2, k2, v_aug, biases, qmasks, sinks=None):
    res = []
    for h in range(2):
        qm = q2 * qmasks[h]
        s = lax.dot_general(qm, k2, (((1,), (1,)), ((), ())), preferred_element_type=_F32)
        s = s + biases[h]
        m = jnp.max(s, axis=-1, keepdims=True)
        if sinks is not None:
            m = jnp.maximum(m, sinks[h])
        p = jnp.exp(s - m).astype(_BF16)
        pv = jnp.dot(p, v_aug, preferred_element_type=_F32)
        res.append((pv[:, :LANES], pv[:, LANES:], m))
    return res


def _variant(qb, nq):
    return jnp.where(qb == 0, 0, jnp.where(qb == nq - 1, 2, 1))


def _loop(n, body, per_iter):
    assert n % per_iter == 0

    def outer(io, c):
        for u in range(per_iter):
            body(io * per_iter + u)
        return c

    lax.fori_loop(0, n // per_iter, outer, 0)


def _attn_a_kernel(q_ref, k_ref, v_ref, bias_ref, o_ref,
                   q4, k4, v4, o1, m1, d1, o2, m2, d2, o3, m3, d3, *, s_len, win):
    kw = Q_BLOCK + 2 * win
    n4 = s_len // 4
    lo, qmasks = _lane_masks()
    ones = jnp.ones((kw, LANES), _BF16)

    def unit(q2, k2, v2, branch, var):
        v_aug = jnp.concatenate([v2.astype(_BF16), ones], axis=1)
        r = _pair_unit(q2.astype(_BF16), k2.astype(_BF16), v_aug,
                       (bias_ref[branch, var, 0], bias_ref[branch, var, 1]), qmasks)
        (oa_, da_, ma_), (ob_, db_, mb_) = r
        return (jnp.where(lo, oa_, ob_), jnp.where(lo, ma_, mb_), jnp.where(lo, da_, db_))

    def window(qb, n):
        qs = qb * Q_BLOCK
        ks = jnp.clip(qs - win, 0, n - kw)
        return qs, ks

    nq1 = s_len // Q_BLOCK

    def b1(qb):
        qs, ks = window(qb, s_len)
        qs = pl.multiple_of(qs, Q_BLOCK)
        ks = pl.multiple_of(ks, 64)
        o, m, dn = unit(q_ref[pl.ds(qs, Q_BLOCK), :], k_ref[pl.ds(ks, kw), :],
                        v_ref[pl.ds(ks, kw), :], 0, _variant(qb, nq1))
        o1[pl.ds(qs, Q_BLOCK), :] = o
        m1[pl.ds(qs, Q_BLOCK), :] = m
        d1[pl.ds(qs, Q_BLOCK), :] = dn

    _loop(nq1, b1, UNITS_PER_ITER)

    blk = 256
    nb4 = n4 // blk

    def to4(i):
        r4 = i // nb4
        u0 = (i % nb4) * blk
        dst = pl.multiple_of(r4 * n4 + u0, blk)
        for src, dst_ref in ((q_ref, q4), (k_ref, k4), (v_ref, v4)):
            dst_ref[pl.ds(dst, blk), :] = src[pl.ds(r4 + 4 * u0, blk, stride=4), :]

    _loop(4 * nb4, to4, 1)

    nq2 = n4 // Q_BLOCK

    def b2(i):
        r4 = i // nq2
        qb = i % nq2
        qs, ks = window(qb, n4)
        qs = pl.multiple_of(r4 * n4 + qs, Q_BLOCK)
        ks = pl.multiple_of(r4 * n4 + ks, 64)
        o, m, dn = unit(q4[pl.ds(qs, Q_BLOCK), :], k4[pl.ds(ks, kw), :],
                        v4[pl.ds(ks, kw), :], 1, _variant(qb, nq2))
        o2[pl.ds(qs, Q_BLOCK), :] = o
        m2[pl.ds(qs, Q_BLOCK), :] = m
        d2[pl.ds(qs, Q_BLOCK), :] = dn

    _loop(4 * nq2, b2, UNITS_PER_ITER)

    n16 = s_len // 16
    nq3 = n16 // Q_BLOCK

    def b3(i):
        r16 = i // nq3
        qb = i % nq3
        base = (r16 % 4) * n4 + r16 // 4
        qs, ks = window(qb, n16)
        qrow = base + 4 * qs
        krow = base + 4 * ks
        o, m, dn = unit(q4[pl.ds(qrow, Q_BLOCK, stride=4), :], k4[pl.ds(krow, kw, stride=4), :],
                        v4[pl.ds(krow, kw, stride=4), :], 2, _variant(qb, nq3))
        o3[pl.ds(qrow, Q_BLOCK, stride=4), :] = o
        m3[pl.ds(qrow, Q_BLOCK, stride=4), :] = m
        d3[pl.ds(qrow, Q_BLOCK, stride=4), :] = dn

    _loop(16 * nq3, b3, UNITS_PER_ITER)

    def mix(i):
        r4 = i // nq2
        u0 = (i % nq2) * Q_BLOCK
        nat = pl.ds(r4 + 4 * u0, Q_BLOCK, stride=4)
        by4 = pl.ds(pl.multiple_of(r4 * n4 + u0, Q_BLOCK), Q_BLOCK)
        ma, mb, mc = m1[nat, :], m2[by4, :], m3[by4, :]
        mx = jnp.maximum(jnp.maximum(ma, mb), mc)
        ea, eb, ec = jnp.exp(ma - mx), jnp.exp(mb - mx), jnp.exp(mc - mx)
        num = ea * o1[nat, :] + eb * o2[by4, :] + ec * o3[by4, :]
        den = ea * d1[nat, :] + eb * d2[by4, :] + ec * d3[by4, :]
        o_ref[nat, :] = num / den

    _loop(4 * nq2, mix, 2)


def _attn_a(qa, ka, va, tables):
    b, s, c = qa.shape
    n_pairs = c // LANES
    win = DILATED_PAIRS[0][0] // 2
    kw = Q_BLOCK + 2 * win
    blk = pl.BlockSpec((None, s, LANES), lambda bi, p: (bi, 0, p))
    bias_spec = pl.BlockSpec((3, 3, 2, Q_BLOCK, kw), lambda bi, p: (0, 0, p, 0, 0))
    kern = functools.partial(_attn_a_kernel, s_len=s, win=win)
    return pl.pallas_call(
        kern,
        grid=(b, n_pairs),
        in_specs=[blk, blk, blk, bias_spec],
        out_specs=blk,
        out_shape=jax.ShapeDtypeStruct((b, s, c), _F32),
        scratch_shapes=[pltpu.VMEM((s, LANES), _F32) for _ in range(12)],
        compiler_params=_compiler_params(2),
        name="attn_a",
    )(qa, ka, va, tables)


def _attn_b_kernel(q_ref, k_ref, v_ref, bias_ref, sink_ref, o_ref, *, s_len, win, group):
    kw = Q_BLOCK + 2 * win
    nq = s_len // Q_BLOCK
    lo, qmasks = _lane_masks()
    ones = jnp.ones((kw, LANES), _BF16)
    g = pl.program_id(1)

    def body(qb):
        qs = pl.multiple_of(qb * Q_BLOCK, Q_BLOCK)
        ks = pl.multiple_of(jnp.clip(qs - win, 0, s_len - kw), 64)
        var = _variant(qb, nq)
        k2 = k_ref[pl.ds(ks, kw), :]
        v_aug = jnp.concatenate([v_ref[pl.ds(ks, kw), :], ones], axis=1)
        for j in range(group // 2):
            cols = slice(j * LANES, (j + 1) * LANES)
            sinks = (sink_ref[g * group + 2 * j], sink_ref[g * group + 2 * j + 1])
            r = _pair_unit(q_ref[pl.ds(qs, Q_BLOCK), cols], k2, v_aug,
                           (bias_ref[var, 2 * j], bias_ref[var, 2 * j + 1]), qmasks, sinks)
            (oa_, da_, ma_), (ob_, db_, mb_) = r
            o = jnp.where(lo, oa_, ob_)
            m = jnp.where(lo, ma_, mb_)
            sk = jnp.where(lo, sinks[0], sinks[1])
            den = jnp.where(lo, da_, db_) + jnp.exp(sk - m)
            o_ref[pl.ds(qs, Q_BLOCK), cols] = (o / den).astype(o_ref.dtype)

    _loop(nq, body, UNITS_PER_ITER // 2)


def _attn_b(qb, kbd, vbd, tables, sink):
    b, s, c = qb.shape
    group = N_HEADS_B // N_KV_B
    gw = group * HEAD_DIM
    kw = Q_BLOCK + 2 * WINDOW_B
    qspec = pl.BlockSpec((None, s, gw), lambda bi, g: (bi, 0, g))
    kspec = pl.BlockSpec((None, s, LANES), lambda bi, g: (bi, 0, g))
    bias_spec = pl.BlockSpec((3, group, Q_BLOCK, kw), lambda bi, g: (0, g, 0, 0))
    kern = functools.partial(_attn_b_kernel, s_len=s, win=WINDOW_B, group=group)
    return pl.pallas_call(
        kern,
        grid=(b, N_KV_B),
        in_specs=[qspec, kspec, kspec, bias_spec, pl.BlockSpec(memory_space=pltpu.SMEM)],
        out_specs=qspec,
        out_shape=jax.ShapeDtypeStruct((b, s, c), _BF16),
        compiler_params=_compiler_params(2),
        name="attn_b",
    )(qb, kbd, vbd, tables, sink)


_MIX_A = N_HEADS_A * HEAD_DIM
_MIX_B = N_HEADS_B * HEAD_DIM
_PROJ_WIDTHS = (_MIX_A, _MIX_A, _MIX_A, _MIX_B, N_KV_B * LANES, N_KV_B * LANES)
_PROJ_DTYPES = (_F32, _F32, _F32, _BF16, _BF16, _BF16)


def _prep_w_in(w_in):
    scale = HEAD_DIM ** -0.5
    kvw = N_KV_B * HEAD_DIM
    offs = np.cumsum((0, _MIX_A, _MIX_A, _MIX_A, _MIX_B, kvw, kvw))
    qa, ka, va, qb, kb, vb = (w_in[:, :, offs[i]:offs[i + 1]] for i in range(6))

    def dup(w):
        heads = [w[:, :, h * HEAD_DIM:(h + 1) * HEAD_DIM] for h in range(N_KV_B)]
        return jnp.concatenate([x for h in heads for x in (h, h)], axis=-1)

    return jnp.concatenate([qa * scale, ka, va, qb * scale, dup(kb), dup(vb)],
                           axis=-1).astype(_BF16)


def _trunk(x, tables_a, tables_b, p):
    b, s, d = x.shape
    t = b * s
    xf = x.reshape(t, d)
    depth = p["w_in"].shape[0]
    for l in range(depth):
        x1, qa, ka, va, qb, kbd, vbd = _ffn_in(
            xf, l, p["ffn1_norm"], p["ffn1_w_gate"], p["ffn1_w_up"], p["ffn1_w_down"],
            p["attn_norm"], p["w_in"], _PROJ_WIDTHS, _PROJ_DTYPES)
        sh = lambda a: a.reshape(b, s, a.shape[1])
        oa = _attn_a(sh(qa), sh(ka), sh(va), tables_a)
        ob = _attn_b(sh(qb), sh(kbd), sh(vbd), tables_b, p["sink"][l])
        xf = _out_ffn(x1, oa.reshape(t, -1), ob.reshape(t, -1), l, p["w_out_a"], p["w_out_b"],
                      p["ffn2_norm"], p["ffn2_w_gate"], p["ffn2_w_up"], p["ffn2_w_down"],
                      p["final_norm"], final_norm=(l == depth - 1))
    return xf.reshape(b, s, d)


def kernel(x_prompt, x_sample, rel_table, ffn1_norm, ffn1_w_gate, ffn1_w_up, ffn1_w_down,
           attn_norm, w_in, w_out, sink, ffn2_norm, ffn2_w_gate, ffn2_w_up, ffn2_w_down,
           final_norm):
    depth, d = ffn1_norm.shape
    assert all(w // (2 * dd) == DILATED_PAIRS[0][0] // 2 for w, dd in DILATED_PAIRS)
    assert tuple(dd for _, dd in DILATED_PAIRS) == (1, 4, 16)
    p = {
        "ffn1_norm": ffn1_norm.reshape(depth, 1, d),
        "ffn1_w_gate": ffn1_w_gate.astype(_BF16),
        "ffn1_w_up": ffn1_w_up.astype(_BF16),
        "ffn1_w_down": ffn1_w_down.astype(_BF16),
        "attn_norm": attn_norm.reshape(depth, 1, d),
        "w_in": _prep_w_in(w_in),
        "w_out_a": w_out[:, :_MIX_A].astype(_BF16),
        "w_out_b": w_out[:, _MIX_A:].astype(_BF16),
        "sink": sink.astype(_F32),
        "ffn2_norm": ffn2_norm.reshape(depth, 1, d),
        "ffn2_w_gate": ffn2_w_gate.astype(_BF16),
        "ffn2_w_up": ffn2_w_up.astype(_BF16),
        "ffn2_w_down": ffn2_w_down.astype(_BF16),
        "final_norm": final_norm.reshape(1, d),
    }
    tables_a = jnp.stack([_band_tables(rel_table, 0, N_HEADS_A, w // (2 * dd), dd)
                          for w, dd in DILATED_PAIRS], axis=0)
    tables_b = _band_tables(rel_table, N_HEADS_A, N_HEADS_B, WINDOW_B, 1)
    return (_trunk(x_prompt, tables_a, tables_b, p),
            _trunk(x_sample, tables_a, tables_b, p))
```

```python
import functools

import numpy as np
import jax
import jax.numpy as jnp
from jax import lax
from jax.experimental import pallas as pl
from jax.experimental.pallas import tpu as pltpu

HEAD_DIM = 64
N_HEADS_A = 8
N_HEADS_B = 8
N_KV_B = 2
DILATED_PAIRS = ((128, 1), (512, 4), (2048, 16))
WINDOW_B = 128
N_BUCKETS = 32
MAX_DISTANCE = 1024
EPS = 1e-6
NEG_INF = -1e30

LANES = 128
Q_BLOCK = 128
TOKEN_TILE = 512
FF_CHUNK = 256
UNITS_PER_ITER = 32
VMEM_LIMIT_BYTES = 56 * 1024 * 1024

_F32 = jnp.float32
_BF16 = jnp.bfloat16


def _rms(x, g):
    y = x * lax.rsqrt(jnp.mean(x * x, axis=-1, keepdims=True) + EPS)
    return y * g


def _swiglu_into(h, wg_ref, wu_ref, a_scr, d_ff):
    for c0 in range(0, d_ff, FF_CHUNK):
        c1 = min(c0 + FF_CHUNK, d_ff)
        g = jnp.dot(h, wg_ref[:, c0:c1], preferred_element_type=_F32)
        u = jnp.dot(h, wu_ref[:, c0:c1], preferred_element_type=_F32)
        a_scr[:, c0:c1] = (g * (1.0 / (1.0 + jnp.exp(-g))) * u).astype(_BF16)


def _const_spec(shape):
    nd = len(shape)
    return pl.BlockSpec(shape, lambda *_: (0,) * nd, pipeline_mode=pl.Buffered(1))


def _layer_spec(arr, layer):
    nd = arr.ndim - 1
    return pl.BlockSpec((None,) + arr.shape[1:], lambda *_: (layer,) + (0,) * nd,
                        pipeline_mode=pl.Buffered(1))


def _compiler_params(n_grid):
    return pltpu.CompilerParams(
        dimension_semantics=("arbitrary",) * n_grid,
        vmem_limit_bytes=VMEM_LIMIT_BYTES)


def _ffn_in_kernel(x_ref, g1_ref, wg_ref, wu_ref, wd_ref, g2_ref, win_ref,
                   x1_ref, *rest, d_ff, widths):
    out_refs, a_scr = rest[:-1], rest[-1]
    x = x_ref[...]
    h = _rms(x, g1_ref[...]).astype(_BF16)
    _swiglu_into(h, wg_ref, wu_ref, a_scr, d_ff)
    y = jnp.dot(a_scr[...], wd_ref[...], preferred_element_type=_F32)
    x1 = x + 0.5 * y
    x1_ref[...] = x1
    h2 = _rms(x1, g2_ref[...]).astype(_BF16)
    c0 = 0
    for o_ref, w in zip(out_refs, widths):
        p = jnp.dot(h2, win_ref[:, c0:c0 + w], preferred_element_type=_F32)
        o_ref[...] = p.astype(o_ref.dtype)
        c0 += w


def _ffn_in(x, layer, g1, wg, wu, wd, g2, w_in, widths, dtypes):
    t, d = x.shape
    d_ff = wg.shape[-1]
    tm = TOKEN_TILE
    tok = lambda w: pl.BlockSpec((tm, w), lambda i: (i, 0))
    kern = functools.partial(_ffn_in_kernel, d_ff=d_ff, widths=widths)
    return pl.pallas_call(
        kern,
        grid=(t // tm,),
        in_specs=[tok(d)] + [_layer_spec(a, layer) for a in (g1, wg, wu, wd, g2, w_in)],
        out_specs=[tok(d)] + [tok(w) for w in widths],
        out_shape=[jax.ShapeDtypeStruct((t, d), _F32)]
        + [jax.ShapeDtypeStruct((t, w), dt) for w, dt in zip(widths, dtypes)],
        scratch_shapes=[pltpu.VMEM((tm, d_ff), _BF16)],
        compiler_params=_compiler_params(1),
        name="ffn_in",
    )(x, g1, wg, wu, wd, g2, w_in)


def _out_ffn_kernel(x_ref, oa_ref, ob_ref, woa_ref, wob_ref, g_ref, wg_ref, wu_ref, wd_ref,
                    gf_ref, y_ref, a_scr, *, d_ff, final_norm):
    x2 = (x_ref[...]
          + jnp.dot(oa_ref[...].astype(_BF16), woa_ref[...], preferred_element_type=_F32)
          + jnp.dot(ob_ref[...], wob_ref[...], preferred_element_type=_F32))
    h = _rms(x2, g_ref[...]).astype(_BF16)
    _swiglu_into(h, wg_ref, wu_ref, a_scr, d_ff)
    y = jnp.dot(a_scr[...], wd_ref[...], preferred_element_type=_F32)
    x3 = x2 + 0.5 * y
    if final_norm:
        x3 = _rms(x3, gf_ref[...])
    y_ref[...] = x3


def _out_ffn(x1, oa, ob, layer, wo_a, wo_b, g, wg, wu, wd, gf, final_norm):
    t, d = x1.shape
    d_ff = wg.shape[-1]
    tm = TOKEN_TILE
    tok = lambda w: pl.BlockSpec((tm, w), lambda i: (i, 0))
    kern = functools.partial(_out_ffn_kernel, d_ff=d_ff, final_norm=final_norm)
    return pl.pallas_call(
        kern,
        grid=(t // tm,),
        in_specs=[tok(d), tok(oa.shape[1]), tok(ob.shape[1])]
        + [_layer_spec(a, layer) for a in (wo_a, wo_b, g, wg, wu, wd)]
        + [_const_spec((1, d))],
        out_specs=tok(d),
        out_shape=jax.ShapeDtypeStruct((t, d), _F32),
        scratch_shapes=[pltpu.VMEM((tm, d_ff), _BF16)],
        compiler_params=_compiler_params(1),
        name="out_ffn",
    )(x1, oa, ob, wo_a, wo_b, g, wg, wu, wd, gf)


def _rel_bucket(rel):
    nb = N_BUCKETS // 2
    max_exact = nb // 2
    n = np.abs(rel)
    large = max_exact + (np.log(np.maximum(n, 1).astype(np.float32) / max_exact)
                         / np.log(MAX_DISTANCE / max_exact) * (nb - max_exact)).astype(np.int32)
    large = np.minimum(large, nb - 1)
    return ((rel > 0).astype(np.int32) * nb + np.where(n < max_exact, n, large)).astype(np.int32)


def _band_tables(rel_table, head_lo, n_heads, win, dilation):
    kw = Q_BLOCK + 2 * win
    i = np.arange(Q_BLOCK)[:, None]
    j = np.arange(kw)[None, :]
    tab = rel_table[:, head_lo:head_lo + n_heads].astype(_F32)
    out = []
    for off in (0, win, 2 * win):
        rel = j - i - off
        bucket = jnp.asarray(_rel_bucket(rel * dilation))[None]
        b = jnp.zeros((n_heads, Q_BLOCK, kw), _F32)
        for kb in range(N_BUCKETS):
            b = jnp.where(bucket == kb, tab[kb][:, None, None], b)
        out.append(jnp.where(jnp.asarray(np.abs(rel) <= win)[None], b, NEG_INF))
    return jnp.stack(out, axis=0)


def _lane_masks():
    lane = lax.broadcasted_iota(jnp.int32, (1, LANES), 1)
    lo = lane < HEAD_DIM
    return lo, (lo.astype(_BF16), (~lo).astype(_BF16))


def _pair_unit(q2, k2, v_aug, biases, qmasks, sinks=None):
    res = []
    for h in range(2):
        qm = q2 * qmasks[h]
        s = lax.dot_general(qm, k2, (((1,), (1,)), ((), ())), preferred_element_type=_F32)
        s = s + biases[h]
        m = jnp.max(s, axis=-1, keepdims=True)
        if sinks is not None:
            m = jnp.maximum(m, sinks[h])
        p = jnp.exp(s - m).astype(_BF16)
        pv = jnp.dot(p, v_aug, preferred_element_type=_F32)
        res.append((pv[:, :LANES], pv[:, LANES:], m))
    return res


def _variant(qb, nq):
    return jnp.where(qb == 0, 0, jnp.where(qb == nq - 1, 2, 1))


def _loop(n, body, per_iter):
    assert n % per_iter == 0

    def outer(io, c):
        for u in range(per_iter):
            body(io * per_iter + u)
        return c

    lax.fori_loop(0, n // per_iter, outer, 0)


def _attn_a_kernel(q_ref, k_ref, v_ref, bias_ref, o_ref,
                   q4, k4, v4, o1, m1, d1, o2, m2, d2, o3, m3, d3, *, s_len, win):
    kw = Q_BLOCK + 2 * win
    n4 = s_len // 4
    lo, qmasks = _lane_masks()
    ones = jnp.ones((kw, LANES), _BF16)

    def unit(q2, k2, v2, branch, var):
        v_aug = jnp.concatenate([v2.astype(_BF16), ones], axis=1)
        r = _pair_unit(q2.astype(_BF16), k2.astype(_BF16), v_aug,
                       (bias_ref[branch, var, 0], bias_ref[branch, var, 1]), qmasks)
        (oa_, da_, ma_), (ob_, db_, mb_) = r
        return (jnp.where(lo, oa_, ob_), jnp.where(lo, ma_, mb_), jnp.where(lo, da_, db_))

    def window(qb, n):
        qs = qb * Q_BLOCK
        ks = jnp.clip(qs - win, 0, n - kw)
        return qs, ks

    nq1 = s_len // Q_BLOCK

    def b1(qb):
        qs, ks = window(qb, s_len)
        qs = pl.multiple_of(qs, Q_BLOCK)
        ks = pl.multiple_of(ks, 64)
        o, m, dn = unit(q_ref[pl.ds(qs, Q_BLOCK), :], k_ref[pl.ds(ks, kw), :],
                        v_ref[pl.ds(ks, kw), :], 0, _variant(qb, nq1))
        o1[pl.ds(qs, Q_BLOCK), :] = o
        m1[pl.ds(qs, Q_BLOCK), :] = m
        d1[pl.ds(qs, Q_BLOCK), :] = dn

    _loop(nq1, b1, UNITS_PER_ITER)

    blk = 256
    nb4 = n4 // blk

    def to4(i):
        r4 = i // nb4
        u0 = (i % nb4) * blk
        dst = pl.multiple_of(r4 * n4 + u0, blk)
        for src, dst_ref in ((q_ref, q4), (k_ref, k4), (v_ref, v4)):
            dst_ref[pl.ds(dst, blk), :] = src[pl.ds(r4 + 4 * u0, blk, stride=4), :]

    _loop(4 * nb4, to4, 1)

    nq2 = n4 // Q_BLOCK

    def b2(i):
        r4 = i // nq2
        qb = i % nq2
        qs, ks = window(qb, n4)
        qs = pl.multiple_of(r4 * n4 + qs, Q_BLOCK)
        ks = pl.multiple_of(r4 * n4 + ks, 64)
        o, m, dn = unit(q4[pl.ds(qs, Q_BLOCK), :], k4[pl.ds(ks, kw), :],
                        v4[pl.ds(ks, kw), :], 1, _variant(qb, nq2))
        o2[pl.ds(qs, Q_BLOCK), :] = o
        m2[pl.ds(qs, Q_BLOCK), :] = m
        d2[pl.ds(qs, Q_BLOCK), :] = dn

    _loop(4 * nq2, b2, UNITS_PER_ITER)

    n16 = s_len // 16
    nq3 = n16 // Q_BLOCK

    def b3(i):
        r16 = i // nq3
        qb = i % nq3
        base = (r16 % 4) * n4 + r16 // 4
        qs, ks = window(qb, n16)
        qrow = base + 4 * qs
        krow = base + 4 * ks
        o, m, dn = unit(q4[pl.ds(qrow, Q_BLOCK, stride=4), :], k4[pl.ds(krow, kw, stride=4), :],
                        v4[pl.ds(krow, kw, stride=4), :], 2, _variant(qb, nq3))
        o3[pl.ds(qrow, Q_BLOCK, stride=4), :] = o
        m3[pl.ds(qrow, Q_BLOCK, stride=4), :] = m
        d3[pl.ds(qrow, Q_BLOCK, stride=4), :] = dn

    _loop(16 * nq3, b3, UNITS_PER_ITER)

    def mix(i):
        r4 = i // nq2
        u0 = (i % nq2) * Q_BLOCK
        nat = pl.ds(r4 + 4 * u0, Q_BLOCK, stride=4)
        by4 = pl.ds(pl.multiple_of(r4 * n4 + u0, Q_BLOCK), Q_BLOCK)
        ma, mb, mc = m1[nat, :], m2[by4, :], m3[by4, :]
        mx = jnp.maximum(jnp.maximum(ma, mb), mc)
        ea, eb, ec = jnp.exp(ma - mx), jnp.exp(mb - mx), jnp.exp(mc - mx)
        num = ea * o1[nat, :] + eb * o2[by4, :] + ec * o3[by4, :]
        den = ea * d1[nat, :] + eb * d2[by4, :] + ec * d3[by4, :]
        o_ref[nat, :] = num / den

    _loop(4 * nq2, mix, 2)


def _attn_a(qa, ka, va, tables):
    b, s, c = qa.shape
    n_pairs = c // LANES
    win = DILATED_PAIRS[0][0] // 2
    kw = Q_BLOCK + 2 * win
    blk = pl.BlockSpec((None, s, LANES), lambda bi, p: (bi, 0, p))
    bias_spec = pl.BlockSpec((3, 3, 2, Q_BLOCK, kw), lambda bi, p: (0, 0, p, 0, 0))
    kern = functools.partial(_attn_a_kernel, s_len=s, win=win)
    return pl.pallas_call(
        kern,
        grid=(b, n_pairs),
        in_specs=[blk, blk, blk, bias_spec],
        out_specs=blk,
        out_shape=jax.ShapeDtypeStruct((b, s, c), _F32),
        scratch_shapes=[pltpu.VMEM((s, LANES), _F32) for _ in range(12)],
        compiler_params=_compiler_params(2),
        name="attn_a",
    )(qa, ka, va, tables)


def _attn_b_kernel(q_ref, k_ref, v_ref, bias_ref, sink_ref, o_ref, *, s_len, win, group):
    kw = Q_BLOCK + 2 * win
    nq = s_len // Q_BLOCK
    lo, qmasks = _lane_masks()
    ones = jnp.ones((kw, LANES), _BF16)
    g = pl.program_id(1)

    def body(qb):
        qs = pl.multiple_of(qb * Q_BLOCK, Q_BLOCK)
        ks = pl.multiple_of(jnp.clip(qs - win, 0, s_len - kw), 64)
        var = _variant(qb, nq)
        k2 = k_ref[pl.ds(ks, kw), :]
        v_aug = jnp.concatenate([v_ref[pl.ds(ks, kw), :], ones], axis=1)
        for j in range(group // 2):
            cols = slice(j * LANES, (j + 1) * LANES)
            sinks = (sink_ref[g * group + 2 * j], sink_ref[g * group + 2 * j + 1])
            r = _pair_unit(q_ref[pl.ds(qs, Q_BLOCK), cols], k2, v_aug,
                           (bias_ref[var, 2 * j], bias_ref[var, 2 * j + 1]), qmasks, sinks)
            (oa_, da_, ma_), (ob_, db_, mb_) = r
            o = jnp.where(lo, oa_, ob_)
            m = jnp.where(lo, ma_, mb_)
            sk = jnp.where(lo, sinks[0], sinks[1])
            den = jnp.where(lo, da_, db_) + jnp.exp(sk - m)
            o_ref[pl.ds(qs, Q_BLOCK), cols] = (o / den).astype(o_ref.dtype)

    _loop(nq, body, UNITS_PER_ITER // 2)


def _attn_b(qb, kbd, vbd, tables, sink):
    b, s, c = qb.shape
    group = N_HEADS_B // N_KV_B
    gw = group * HEAD_DIM
    kw = Q_BLOCK + 2 * WINDOW_B
    qspec = pl.BlockSpec((None, s, gw), lambda bi, g: (bi, 0, g))
    kspec = pl.BlockSpec((None, s, LANES), lambda bi, g: (bi, 0, g))
    bias_spec = pl.BlockSpec((3, group, Q_BLOCK, kw), lambda bi, g: (0, g, 0, 0))
    kern = functools.partial(_attn_b_kernel, s_len=s, win=WINDOW_B, group=group)
    return pl.pallas_call(
        kern,
        grid=(b, N_KV_B),
        in_specs=[qspec, kspec, kspec, bias_spec, pl.BlockSpec(memory_space=pltpu.SMEM)],
        out_specs=qspec,
        out_shape=jax.ShapeDtypeStruct((b, s, c), _BF16),
        compiler_params=_compiler_params(2),
        name="attn_b",
    )(qb, kbd, vbd, tables, sink)


_MIX_A = N_HEADS_A * HEAD_DIM
_MIX_B = N_HEADS_B * HEAD_DIM
_PROJ_WIDTHS = (_MIX_A, _MIX_A, _MIX_A, _MIX_B, N_KV_B * LANES, N_KV_B * LANES)
_PROJ_DTYPES = (_F32, _F32, _F32, _BF16, _BF16, _BF16)


def _prep_w_in(w_in):
    scale = HEAD_DIM ** -0.5
    kvw = N_KV_B * HEAD_DIM
    offs = np.cumsum((0, _MIX_A, _MIX_A, _MIX_A, _MIX_B, kvw, kvw))
    qa, ka, va, qb, kb, vb = (w_in[:, :, offs[i]:offs[i + 1]] for i in range(6))

    def dup(w):
        heads = [w[:, :, h * HEAD_DIM:(h + 1) * HEAD_DIM] for h in range(N_KV_B)]
        return jnp.concatenate([x for h in heads for x in (h, h)], axis=-1)

    return jnp.concatenate([qa * scale, ka, va, qb * scale, dup(kb), dup(vb)],
                           axis=-1).astype(_BF16)


def _trunk(x, tables_a, tables_b, p):
    b, s, d = x.shape
    t = b * s
    xf = x.reshape(t, d)
    depth = p["w_in"].shape[0]
    for l in range(depth):
        x1, qa, ka, va, qb, kbd, vbd = _ffn_in(
            xf, l, p["ffn1_norm"], p["ffn1_w_gate"], p["ffn1_w_up"], p["ffn1_w_down"],
            p["attn_norm"], p["w_in"], _PROJ_WIDTHS, _PROJ_DTYPES)
        sh = lambda a: a.reshape(b, s, a.shape[1])
        oa = _attn_a(sh(qa), sh(ka), sh(va), tables_a)
        ob = _attn_b(sh(qb), sh(kbd), sh(vbd), tables_b, p["sink"][l])
        xf = _out_ffn(x1, oa.reshape(t, -1), ob.reshape(t, -1), l, p["w_out_a"], p["w_out_b"],
                      p["ffn2_norm"], p["ffn2_w_gate"], p["ffn2_w_up"], p["ffn2_w_down"],
                      p["final_norm"], final_norm=(l == depth - 1))
    return xf.reshape(b, s, d)


def kernel(x_prompt, x_sample, rel_table, ffn1_norm, ffn1_w_gate, ffn1_w_up, ffn1_w_down,
           attn_norm, w_in, w_out, sink, ffn2_norm, ffn2_w_gate, ffn2_w_up, ffn2_w_down,
           final_norm):
    depth, d = ffn1_norm.shape
    assert all(w // (2 * dd) == DILATED_PAIRS[0][0] // 2 for w, dd in DILATED_PAIRS)
    assert tuple(dd for _, dd in DILATED_PAIRS) == (1, 4, 16)
    p = {
        "ffn1_norm": ffn1_norm.reshape(depth, 1, d),
        "ffn1_w_gate": ffn1_w_gate.astype(_BF16),
        "ffn1_w_up": ffn1_w_up.astype(_BF16),
        "ffn1_w_down": ffn1_w_down.astype(_BF16),
        "attn_norm": attn_norm.reshape(depth, 1, d),
        "w_in": _prep_w_in(w_in),
        "w_out_a": w_out[:, :_MIX_A].astype(_BF16),
        "w_out_b": w_out[:, _MIX_A:].astype(_BF16),
        "sink": sink.astype(_F32),
        "ffn2_norm": ffn2_norm.reshape(depth, 1, d),
        "ffn2_w_gate": ffn2_w_gate.astype(_BF16),
        "ffn2_w_up": ffn2_w_up.astype(_BF16),
        "ffn2_w_down": ffn2_w_down.astype(_BF16),
        "final_norm": final_norm.reshape(1, d),
    }
    tables_a = jnp.stack([_band_tables(rel_table, 0, N_HEADS_A, w // (2 * dd), dd)
                          for w, dd in DILATED_PAIRS], axis=0)
    tables_b = _band_tables(rel_table, N_HEADS_A, N_HEADS_B, WINDOW_B, 1)
    return (_trunk(x_prompt, tables_a, tables_b, p),
            _trunk(x_sample, tables_a, tables_b, p))
```

```python
import functools

import numpy as np
import jax
import jax.numpy as jnp
from jax import lax
from jax.experimental import pallas as pl
from jax.experimental.pallas import tpu as pltpu

HEAD_DIM = 64
N_HEADS_A = 8
N_HEADS_B = 8
N_KV_B = 2
DILATED_PAIRS = ((128, 1), (512, 4), (2048, 16))
WINDOW_B = 128
N_BUCKETS = 32
MAX_DISTANCE = 1024
EPS = 1e-6
NEG_INF = -1e30

LANES = 128
Q_BLOCK = 128
TOKEN_TILE = 512
FF_CHUNK = 256
UNITS_PER_ITER = 32
VMEM_LIMIT_BYTES = 56 * 1024 * 1024

_F32 = jnp.float32
_BF16 = jnp.bfloat16


def _rms(x, g):
    y = x * lax.rsqrt(jnp.mean(x * x, axis=-1, keepdims=True) + EPS)
    return y * g


def _swiglu_into(h, wg_ref, wu_ref, a_scr, d_ff):
    for c0 in range(0, d_ff, FF_CHUNK):
        c1 = min(c0 + FF_CHUNK, d_ff)
        g = jnp.dot(h, wg_ref[:, c0:c1], preferred_element_type=_F32)
        u = jnp.dot(h, wu_ref[:, c0:c1], preferred_element_type=_F32)
        a_scr[:, c0:c1] = (g * (1.0 / (1.0 + jnp.exp(-g))) * u).astype(_BF16)


def _const_spec(shape):
    nd = len(shape)
    return pl.BlockSpec(shape, lambda *_: (0,) * nd, pipeline_mode=pl.Buffered(1))


def _layer_spec(arr, layer):
    nd = arr.ndim - 1
    return pl.BlockSpec((None,) + arr.shape[1:], lambda *_: (layer,) + (0,) * nd,
                        pipeline_mode=pl.Buffered(1))


def _compiler_params(n_grid):
    return pltpu.CompilerParams(
        dimension_semantics=("arbitrary",) * n_grid,
        vmem_limit_bytes=VMEM_LIMIT_BYTES)


def _ffn_in_kernel(x_ref, g1_ref, wg_ref, wu_ref, wd_ref, g2_ref, win_ref,
                   x1_ref, *rest, d_ff, widths):
    out_refs, a_scr = rest[:-1], rest[-1]
    x = x_ref[...]
    h = _rms(x, g1_ref[...]).astype(_BF16)
    _swiglu_into(h, wg_ref, wu_ref, a_scr, d_ff)
    y = jnp.dot(a_scr[...], wd_ref[...], preferred_element_type=_F32)
    x1 = x + 0.5 * y
    x1_ref[...] = x1
    h2 = _rms(x1, g2_ref[...]).astype(_BF16)
    c0 = 0
    for o_ref, w in zip(out_refs, widths):
        p = jnp.dot(h2, win_ref[:, c0:c0 + w], preferred_element_type=_F32)
        o_ref[...] = p.astype(o_ref.dtype)
        c0 += w


def _ffn_in(x, layer, g1, wg, wu, wd, g2, w_in, widths, dtypes):
    t, d = x.shape
    d_ff = wg.shape[-1]
    tm = TOKEN_TILE
    tok = lambda w: pl.BlockSpec((tm, w), lambda i: (i, 0))
    kern = functools.partial(_ffn_in_kernel, d_ff=d_ff, widths=widths)
    return pl.pallas_call(
        kern,
        grid=(t // tm,),
        in_specs=[tok(d)] + [_layer_spec(a, layer) for a in (g1, wg, wu, wd, g2, w_in)],
        out_specs=[tok(d)] + [tok(w) for w in widths],
        out_shape=[jax.ShapeDtypeStruct((t, d), _F32)]
        + [jax.ShapeDtypeStruct((t, w), dt) for w, dt in zip(widths, dtypes)],
        scratch_shapes=[pltpu.VMEM((tm, d_ff), _BF16)],
        compiler_params=_compiler_params(1),
        name="ffn_in",
    )(x, g1, wg, wu, wd, g2, w_in)


def _out_ffn_kernel(x_ref, oa_ref, ob_ref, woa_ref, wob_ref, g_ref, wg_ref, wu_ref, wd_ref,
                    gf_ref, y_ref, a_scr, *, d_ff, final_norm):
    x2 = (x_ref[...]
          + jnp.dot(oa_ref[...].astype(_BF16), woa_ref[...], preferred_element_type=_F32)
          + jnp.dot(ob_ref[...], wob_ref[...], preferred_element_type=_F32))
    h = _rms(x2, g_ref[...]).astype(_BF16)
    _swiglu_into(h, wg_ref, wu_ref, a_scr, d_ff)
    y = jnp.dot(a_scr[...], wd_ref[...], preferred_element_type=_F32)
    x3 = x2 + 0.5 * y
    if final_norm:
        x3 = _rms(x3, gf_ref[...])
    y_ref[...] = x3


def _out_ffn(x1, oa, ob, layer, wo_a, wo_b, g, wg, wu, wd, gf, final_norm):
    t, d = x1.shape
    d_ff = wg.shape[-1]
    tm = TOKEN_TILE
    tok = lambda w: pl.BlockSpec((tm, w), lambda i: (i, 0))
    kern = functools.partial(_out_ffn_kernel, d_ff=d_ff, final_norm=final_norm)
    return pl.pallas_call(
        kern,
        grid=(t // tm,),
        in_specs=[tok(d), tok(oa.shape[1]), tok(ob.shape[1])]
        + [_layer_spec(a, layer) for a in (wo_a, wo_b, g, wg, wu, wd)]
        + [_const_spec((1, d))],
        out_specs=tok(d),
        out_shape=jax.ShapeDtypeStruct((t, d), _F32),
        scratch_shapes=[pltpu.VMEM((tm, d_ff), _BF16)],
        compiler_params=_compiler_params(1),
        name="out_ffn",
    )(x1, oa, ob, wo_a, wo_b, g, wg, wu, wd, gf)


def _rel_bucket(rel):
    nb = N_BUCKETS // 2
    max_exact = nb // 2
    n = np.abs(rel)
    large = max_exact + (np.log(np.maximum(n, 1).astype(np.float32) / max_exact)
                         / np.log(MAX_DISTANCE / max_exact) * (nb - max_exact)).astype(np.int32)
    large = np.minimum(large, nb - 1)
    return ((rel > 0).astype(np.int32) * nb + np.where(n < max_exact, n, large)).astype(np.int32)


def _band_tables(rel_table, head_lo, n_heads, win, dilation):
    kw = Q_BLOCK + 2 * win
    i = np.arange(Q_BLOCK)[:, None]
    j = np.arange(kw)[None, :]
    tab = rel_table[:, head_lo:head_lo + n_heads].astype(_F32)
    out = []
    for off in (0, win, 2 * win):
        rel = j - i - off
        bucket = jnp.asarray(_rel_bucket(rel * dilation))[None]
        b = jnp.zeros((n_heads, Q_BLOCK, kw), _F32)
        for kb in range(N_BUCKETS):
            b = jnp.where(bucket == kb, tab[kb][:, None, None], b)
        out.append(jnp.where(jnp.asarray(np.abs(rel) <= win)[None], b, NEG_INF))
    return jnp.stack(out, axis=0)


def _lane_masks():
    lane = lax.broadcasted_iota(jnp.int32, (1, LANES), 1)
    lo = lane < HEAD_DIM
    return lo, (lo.astype(_BF16), (~lo).astype(_BF16))


def _pair_unit(q2, k2, v_aug, biases, qmasks, sinks=None):
    res = []
    for h in range(2):
        qm = q2 * qmasks[h]
        s = lax.dot_general(qm, k2, (((1,), (1,)), ((), ())), preferred_element_type=_F32)
        s = s + biases[h]
        m = jnp.max(s, axis=-1, keepdims=True)
        if sinks is not None:
            m = jnp.maximum(m, sinks[h])
        p = jnp.exp(s - m).astype(_BF16)
        pv = jnp.dot(p, v_aug, preferred_element_type=_F32)
        res.append((pv[:, :LANES], pv[:, LANES:], m))
    return res


def _variant(qb, nq):
    return jnp.where(qb == 0, 0, jnp.where(qb == nq - 1, 2, 1))


def _loop(n, body, per_iter):
    assert n % per_iter == 0

    def outer(io, c):
        for u in range(per_iter):
            body(io * per_iter + u)
        return c

    lax.fori_loop(0, n // per_iter, outer, 0)


def _attn_a_kernel(q_ref, k_ref, v_ref, bias_ref, o_ref,
                   q4, k4, v4, o1, m1, d1, o2, m2, d2, o3, m3, d3, *, s_len, win):
    kw = Q_BLOCK + 2 * win
    n4 = s_len // 4
    lo, qmasks = _lane_masks()
    ones = jnp.ones((kw, LANES), _BF16)

    def unit(q2, k2, v2, branch, var):
        v_aug = jnp.concatenate([v2.astype(_BF16), ones], axis=1)
        r = _pair_unit(q2.astype(_BF16), k2.astype(_BF16), v_aug,
                       (bias_ref[branch, var, 0], bias_ref[branch, var, 1]), qmasks)
        (oa_, da_, ma_), (ob_, db_, mb_) = r
        return (jnp.where(lo, oa_, ob_), jnp.where(lo, ma_, mb_), jnp.where(lo, da_, db_))

    def window(qb, n):
        qs = qb * Q_BLOCK
        ks = jnp.clip(qs - win, 0, n - kw)
        return qs, ks

    nq1 = s_len // Q_BLOCK

    def b1(qb):
        qs, ks = window(qb, s_len)
        qs = pl.multiple_of(qs, Q_BLOCK)
        ks = pl.multiple_of(ks, 64)
        o, m, dn = unit(q_ref[pl.ds(qs, Q_BLOCK), :], k_ref[pl.ds(ks, kw), :],
                        v_ref[pl.ds(ks, kw), :], 0, _variant(qb, nq1))
        o1[pl.ds(qs, Q_BLOCK), :] = o
        m1[pl.ds(qs, Q_BLOCK), :] = m
        d1[pl.ds(qs, Q_BLOCK), :] = dn

    _loop(nq1, b1, UNITS_PER_ITER)

    blk = 256
    nb4 = n4 // blk

    def to4(i):
        r4 = i // nb4
        u0 = (i % nb4) * blk
        dst = pl.multiple_of(r4 * n4 + u0, blk)
        for src, dst_ref in ((q_ref, q4), (k_ref, k4), (v_ref, v4)):
            dst_ref[pl.ds(dst, blk), :] = src[pl.ds(r4 + 4 * u0, blk, stride=4), :]

    _loop(4 * nb4, to4, 4 * nb4)

    nq2 = n4 // Q_BLOCK

    def b2(i):
        r4 = i // nq2
        qb = i % nq2
        qs, ks = window(qb, n4)
        qs = pl.multiple_of(r4 * n4 + qs, Q_BLOCK)
        ks = pl.multiple_of(r4 * n4 + ks, 64)
        o, m, dn = unit(q4[pl.ds(qs, Q_BLOCK), :], k4[pl.ds(ks, kw), :],
                        v4[pl.ds(ks, kw), :], 1, _variant(qb, nq2))
        o2[pl.ds(qs, Q_BLOCK), :] = o
        m2[pl.ds(qs, Q_BLOCK), :] = m
        d2[pl.ds(qs, Q_BLOCK), :] = dn

    _loop(4 * nq2, b2, UNITS_PER_ITER)

    n16 = s_len // 16
    nq3 = n16 // Q_BLOCK

    def b3(i):
        r16 = i // nq3
        qb = i % nq3
        base = (r16 % 4) * n4 + r16 // 4
        qs, ks = window(qb, n16)
        qrow = base + 4 * qs
        krow = base + 4 * ks
        o, m, dn = unit(q4[pl.ds(qrow, Q_BLOCK, stride=4), :], k4[pl.ds(krow, kw, stride=4), :],
                        v4[pl.ds(krow, kw, stride=4), :], 2, _variant(qb, nq3))
        o3[pl.ds(qrow, Q_BLOCK, stride=4), :] = o
        m3[pl.ds(qrow, Q_BLOCK, stride=4), :] = m
        d3[pl.ds(qrow, Q_BLOCK, stride=4), :] = dn

    _loop(16 * nq3, b3, UNITS_PER_ITER)

    def mix(i):
        r4 = i // nq2
        u0 = (i % nq2) * Q_BLOCK
        nat = pl.ds(r4 + 4 * u0, Q_BLOCK, stride=4)
        by4 = pl.ds(pl.multiple_of(r4 * n4 + u0, Q_BLOCK), Q_BLOCK)
        ma, mb, mc = m1[nat, :], m2[by4, :], m3[by4, :]
        mx = jnp.maximum(jnp.maximum(ma, mb), mc)
        ea, eb, ec = jnp.exp(ma - mx), jnp.exp(mb - mx), jnp.exp(mc - mx)
        num = ea * o1[nat, :] + eb * o2[by4, :] + ec * o3[by4, :]
        den = ea * d1[nat, :] + eb * d2[by4, :] + ec * d3[by4, :]
        o_ref[nat, :] = num / den

    _loop(4 * nq2, mix, 2)


def _attn_a(qa, ka, va, tables):
    b, s, c = qa.shape
    n_pairs = c // LANES
    win = DILATED_PAIRS[0][0] // 2
    kw = Q_BLOCK + 2 * win
    blk = pl.BlockSpec((None, s, LANES), lambda bi, p: (bi, 0, p))
    bias_spec = pl.BlockSpec((3, 3, 2, Q_BLOCK, kw), lambda bi, p: (0, 0, p, 0, 0))
    kern = functools.partial(_attn_a_kernel, s_len=s, win=win)
    return pl.pallas_call(
        kern,
        grid=(b, n_pairs),
        in_specs=[blk, blk, blk, bias_spec],
        out_specs=blk,
        out_shape=jax.ShapeDtypeStruct((b, s, c), _F32),
        scratch_shapes=[pltpu.VMEM((s, LANES), _F32) for _ in range(12)],
        compiler_params=_compiler_params(2),
        name="attn_a",
    )(qa, ka, va, tables)


def _attn_b_kernel(q_ref, k_ref, v_ref, bias_ref, sink_ref, o_ref, *, s_len, win, group):
    kw = Q_BLOCK + 2 * win
    nq = s_len // Q_BLOCK
    lo, qmasks = _lane_masks()
    ones = jnp.ones((kw, LANES), _BF16)
    g = pl.program_id(1)

    def body(qb):
        qs = pl.multiple_of(qb * Q_BLOCK, Q_BLOCK)
        ks = pl.multiple_of(jnp.clip(qs - win, 0, s_len - kw), 64)
        var = _variant(qb, nq)
        k2 = k_ref[pl.ds(ks, kw), :]
        v_aug = jnp.concatenate([v_ref[pl.ds(ks, kw), :], ones], axis=1)
        for j in range(group // 2):
            cols = slice(j * LANES, (j + 1) * LANES)
            sinks = (sink_ref[g * group + 2 * j], sink_ref[g * group + 2 * j + 1])
            r = _pair_unit(q_ref[pl.ds(qs, Q_BLOCK), cols], k2, v_aug,
                           (bias_ref[var, 2 * j], bias_ref[var, 2 * j + 1]), qmasks, sinks)
            (oa_, da_, ma_), (ob_, db_, mb_) = r
            o = jnp.where(lo, oa_, ob_)
            m = jnp.where(lo, ma_, mb_)
            sk = jnp.where(lo, sinks[0], sinks[1])
            den = jnp.where(lo, da_, db_) + jnp.exp(sk - m)
            o_ref[pl.ds(qs, Q_BLOCK), cols] = (o / den).astype(o_ref.dtype)

    _loop(nq, body, UNITS_PER_ITER // 2)


def _attn_b(qb, kbd, vbd, tables, sink):
    b, s, c = qb.shape
    group = N_HEADS_B // N_KV_B
    gw = group * HEAD_DIM
    kw = Q_BLOCK + 2 * WINDOW_B
    qspec = pl.BlockSpec((None, s, gw), lambda bi, g: (bi, 0, g))
    kspec = pl.BlockSpec((None, s, LANES), lambda bi, g: (bi, 0, g))
    bias_spec = pl.BlockSpec((3, group, Q_BLOCK, kw), lambda bi, g: (0, g, 0, 0))
    kern = functools.partial(_attn_b_kernel, s_len=s, win=WINDOW_B, group=group)
    return pl.pallas_call(
        kern,
        grid=(b, N_KV_B),
        in_specs=[qspec, kspec, kspec, bias_spec, pl.BlockSpec(memory_space=pltpu.SMEM)],
        out_specs=qspec,
        out_shape=jax.ShapeDtypeStruct((b, s, c), _BF16),
        compiler_params=_compiler_params(2),
        name="attn_b",
    )(qb, kbd, vbd, tables, sink)


_MIX_A = N_HEADS_A * HEAD_DIM
_MIX_B = N_HEADS_B * HEAD_DIM
_PROJ_WIDTHS = (_MIX_A, _MIX_A, _MIX_A, _MIX_B, N_KV_B * LANES, N_KV_B * LANES)
_PROJ_DTYPES = (_F32, _F32, _F32, _BF16, _BF16, _BF16)


def _prep_w_in(w_in):
    scale = HEAD_DIM ** -0.5
    kvw = N_KV_B * HEAD_DIM
    offs = np.cumsum((0, _MIX_A, _MIX_A, _MIX_A, _MIX_B, kvw, kvw))
    qa, ka, va, qb, kb, vb = (w_in[:, :, offs[i]:offs[i + 1]] for i in range(6))

    def dup(w):
        heads = [w[:, :, h * HEAD_DIM:(h + 1) * HEAD_DIM] for h in range(N_KV_B)]
        return jnp.concatenate([x for h in heads for x in (h, h)], axis=-1)

    return jnp.concatenate([qa * scale, ka, va, qb * scale, dup(kb), dup(vb)],
                           axis=-1).astype(_BF16)


def _trunk(x, tables_a, tables_b, p):
    b, s, d = x.shape
    t = b * s
    xf = x.reshape(t, d)
    depth = p["w_in"].shape[0]
    for l in range(depth):
        x1, qa, ka, va, qb, kbd, vbd = _ffn_in(
            xf, l, p["ffn1_norm"], p["ffn1_w_gate"], p["ffn1_w_up"], p["ffn1_w_down"],
            p["attn_norm"], p["w_in"], _PROJ_WIDTHS, _PROJ_DTYPES)
        sh = lambda a: a.reshape(b, s, a.shape[1])
        oa = _attn_a(sh(qa), sh(ka), sh(va), tables_a)
        ob = _attn_b(sh(qb), sh(kbd), sh(vbd), tables_b, p["sink"][l])
        xf = _out_ffn(x1, oa.reshape(t, -1), ob.reshape(t, -1), l, p["w_out_a"], p["w_out_b"],
                      p["ffn2_norm"], p["ffn2_w_gate"], p["ffn2_w_up"], p["ffn2_w_down"],
                      p["final_norm"], final_norm=(l == depth - 1))
    return xf.reshape(b, s, d)


def kernel(x_prompt, x_sample, rel_table, ffn1_norm, ffn1_w_gate, ffn1_w_up, ffn1_w_down,
           attn_norm, w_in, w_out, sink, ffn2_norm, ffn2_w_gate, ffn2_w_up, ffn2_w_down,
           final_norm):
    depth, d = ffn1_norm.shape
    assert all(w // (2 * dd) == DILATED_PAIRS[0][0] // 2 for w, dd in DILATED_PAIRS)
    assert tuple(dd for _, dd in DILATED_PAIRS) == (1, 4, 16)
    p = {
        "ffn1_norm": ffn1_norm.reshape(depth, 1, d),
        "ffn1_w_gate": ffn1_w_gate.astype(_BF16),
        "ffn1_w_up": ffn1_w_up.astype(_BF16),
        "ffn1_w_down": ffn1_w_down.astype(_BF16),
        "attn_norm": attn_norm.reshape(depth, 1, d),
        "w_in": _prep_w_in(w_in),
        "w_out_a": w_out[:, :_MIX_A].astype(_BF16),
        "w_out_b": w_out[:, _MIX_A:].astype(_BF16),
        "sink": sink.astype(_F32),
        "ffn2_norm": ffn2_norm.reshape(depth, 1, d),
        "ffn2_w_gate": ffn2_w_gate.astype(_BF16),
        "ffn2_w_up": ffn2_w_up.astype(_BF16),
        "ffn2_w_down": ffn2_w_down.astype(_BF16),
        "final_norm": final_norm.reshape(1, d),
    }
    tables_a = jnp.stack([_band_tables(rel_table, 0, N_HEADS_A, w // (2 * dd), dd)
                          for w, dd in DILATED_PAIRS], axis=0)
    tables_b = _band_tables(rel_table, N_HEADS_A, N_HEADS_B, WINDOW_B, 1)
    return (_trunk(x_prompt, tables_a, tables_b, p),
            _trunk(x_sample, tables_a, tables_b, p))
```
